```python
import math
import jax
import jax.numpy as jnp
from jax import lax
import numpy as np

D_MODEL = 1024
BATCH = 4
SEQ = 8192
DEPTH = 4

GRID_W = 64
CTX_LEN = 256
N_BRANCH = 3
BRANCH_WIDTH = 1024
EPS = 1e-6
CONV_K = 3
SSD_HEADS = 16
SSD_HEAD_DIM = 64
SSD_GROUPS = 2
SSD_STATE = 128
SSD_CHUNK = 128
DIFF_HEADS = 8
DIFF_HEAD_DIM = 64
Q_BLOCK = 128
ROPE_BASE = 10000.0
ML_HEADS = 4
ML_QK_DIM = 128
ML_V_DIM = 256
ML_CHUNK = 128

SSD_BC = 2 * SSD_GROUPS * SSD_STATE
SSD_XBC = BRANCH_WIDTH + SSD_BC
ML_QK = 2 * ML_HEADS * ML_QK_DIM
IN_SIZES = (SSD_XBC, BRANCH_WIDTH, 2 * SSD_HEADS,
            BRANCH_WIDTH, BRANCH_WIDTH, BRANCH_WIDTH, BRANCH_WIDTH,
            ML_QK, BRANCH_WIDTH, BRANCH_WIDTH, BRANCH_WIDTH, 2 * ML_HEADS, 2 * ML_HEADS,
            N_BRANCH * D_MODEL)
D_IN = SSD_XBC + 2 * SSD_HEADS + 8 * BRANCH_WIDTH + ML_QK + 4 * ML_HEADS + N_BRANCH * D_MODEL

kernel_name = 'hybrid_ssd_diffattn_mlstm_prefix_dit'


def rms_norm(x, g):
    xf = x.astype(jnp.float32)
    y = xf * lax.rsqrt(jnp.mean(xf * xf, axis=-1, keepdims=True) + EPS)
    return (y * g.astype(jnp.float32)).astype(x.dtype)


def split_cols(t, sizes):
    idx = [int(i) for i in np.cumsum(sizes)[:-1]]
    return jnp.split(t, idx, axis=-1)


def flip(t):
    return jnp.flip(t, axis=1)


def dwconv(u, w, b):
    ch = u.shape[-1]
    y = lax.conv_general_dilated(u, w.astype(u.dtype)[:, None, :], (1,),
                                 ((CONV_K // 2, CONV_K // 2),),
                                 dimension_numbers=('NWC', 'WIO', 'NWC'),
                                 feature_group_count=ch)
    return y + b.astype(u.dtype)


def axial_rope_tables(n_tokens):
    rows = n_tokens // GRID_W
    row = jnp.repeat(jnp.arange(rows, dtype=jnp.float32), GRID_W)
    col = jnp.tile(jnp.arange(GRID_W, dtype=jnp.float32), rows)
    half = DIFF_HEAD_DIM // 2
    inv_freq = ROPE_BASE ** (-jnp.arange(0, half, 2, dtype=jnp.float32) / half)
    ang_r = row[:, None] * inv_freq
    ang_c = col[:, None] * inv_freq
    cos = jnp.concatenate([jnp.cos(ang_r), jnp.cos(ang_r), jnp.cos(ang_c), jnp.cos(ang_c)], axis=-1)
    sin = jnp.concatenate([jnp.sin(ang_r), jnp.sin(ang_r), jnp.sin(ang_c), jnp.sin(ang_c)], axis=-1)
    return cos, sin


def apply_rope(t, cos, sin):
    tr = t.reshape(t.shape[:-1] + (2, 2, DIFF_HEAD_DIM // 4))
    rot = jnp.stack([-tr[..., 1, :], tr[..., 0, :]], axis=-2).reshape(t.shape)
    return (t * cos[:, None, None, :] + rot * sin[:, None, None, :]).astype(t.dtype)


def ssd_scan(x, dt, A, Bm, Cm, s0):
    b, T, H, P = x.shape
    G, N = Bm.shape[-2], Bm.shape[-1]
    R = H // G
    L = SSD_CHUNK
    nc = T // L
    xd = (x * dt[..., None]).reshape(b, nc, L, G, R, P)
    a = (dt * A).reshape(b, nc, L, G, R)
    acs = jnp.cumsum(jnp.moveaxis(a, 2, -1), axis=-1)
    Bc = Bm.reshape(b, nc, L, G, N)
    Cc = Cm.reshape(b, nc, L, G, N)
    tri = jnp.tril(jnp.ones((L, L), dtype=bool))
    seg = acs[..., :, None] - acs[..., None, :]
    decay = jnp.exp(jnp.where(tri, seg, -jnp.inf))
    scores = jnp.einsum('bclgn,bcsgn->bcgls', Cc, Bc)[:, :, :, None] * decay
    y_diag = jnp.einsum('bcgrls,bcsgrp->bclgrp', scores, xd)
    decay_end = jnp.exp(acs[..., -1:] - acs)
    st = jnp.einsum('bclgn,bcgrl,bclgrp->bcgrpn', Bc, decay_end, xd)
    chunk_decay = jnp.exp(acs[..., -1])

    def step(s, inp):
        st_c, cd_c = inp
        return s * cd_c[..., None, None] + st_c, s

    final, prev = lax.scan(step, s0.reshape(b, G, R, P, N),
                           (jnp.moveaxis(st, 1, 0), jnp.moveaxis(chunk_decay, 1, 0)))
    prev = jnp.moveaxis(prev, 0, 1)
    y_off = jnp.einsum('bclgn,bcgrpn,bcgrl->bclgrp', Cc, prev, jnp.exp(acs))
    y = (y_diag + y_off).reshape(b, T, H, P)
    return y.astype(x.dtype), final.reshape(b, H, P, N)


def ssd_branch(xbc, z, dt_raw, xbc_c, z_c, dt_raw_c, conv_w, conv_b, a_log, dt_bias, d_skip, norm_g):
    A = -jnp.exp(a_log.astype(jnp.float32))

    def prep(xbc, dt_raw):
        b, T, _ = xbc.shape
        u = jax.nn.silu(dwconv(xbc, conv_w, conv_b))
        xs, Bm, Cm = split_cols(u, (BRANCH_WIDTH, SSD_BC // 2, SSD_BC // 2))
        dt = jax.nn.softplus(dt_raw.astype(jnp.float32).reshape(b, T, 2, SSD_HEADS)
                             + dt_bias.astype(jnp.float32))
        return (xs.reshape(b, T, SSD_HEADS, SSD_HEAD_DIM), Bm.reshape(b, T, SSD_GROUPS, SSD_STATE),
                Cm.reshape(b, T, SSD_GROUPS, SSD_STATE), dt)

    xs, Bm, Cm, dt = prep(xbc, dt_raw)
    xsc, Bc, Cc, dtc = prep(xbc_c, dt_raw_c)
    b = xs.shape[0]
    s0 = jnp.zeros((b, SSD_HEADS, SSD_HEAD_DIM, SSD_STATE), jnp.float32)
    yc_f, s_f = ssd_scan(xsc, dtc[:, :, 0], A[0], Bc, Cc, s0)
    yc_b, s_b = ssd_scan(flip(xsc), flip(dtc[:, :, 1]), A[1], flip(Bc), flip(Cc), s0)
    y_f, _ = ssd_scan(xs, dt[:, :, 0], A[0], Bm, Cm, s_f)
    y_b, _ = ssd_scan(flip(xs), flip(dt[:, :, 1]), A[1], flip(Bm), flip(Cm), s_b)

    def finish(xs, y_f, y_b_rev, z):
        bb, T = xs.shape[:2]
        y = (y_f + flip(y_b_rev) + d_skip[:, None].astype(xs.dtype) * xs).reshape(bb, T, BRANCH_WIDTH)
        return rms_norm(y * jax.nn.silu(z), norm_g)

    return finish(xs, y_f, y_b, z), finish(xsc, yc_f, yc_b, z_c)


def diff_attend(qb, k_all, v_all, lam):
    s = jnp.einsum('bqhcd,bkhcd->bhcqk', qb, k_all).astype(jnp.float32)
    p = jax.nn.softmax(s, axis=-1)
    a = p[:, :, 0] - lam * p[:, :, 1]
    return jnp.einsum('bhqk,bkhe->bqhe', a.astype(v_all.dtype), v_all)


def diff_branch(q, k, v, z, q_c, k_c, v_c, z_c, qn_g, kn_g, lam_p, subln_g, lam_init, cos, sin):
    scale = DIFF_HEAD_DIM ** -0.5

    def heads(t, g):
        bb, T, _ = t.shape
        return rms_norm(t.reshape(bb, T, DIFF_HEADS, 2, DIFF_HEAD_DIM), g)

    qn = apply_rope(heads(q, qn_g), cos, sin) * scale
    kn = apply_rope(heads(k, kn_g), cos, sin)
    qcn = heads(q_c, qn_g) * scale
    kcn = heads(k_c, kn_g)
    b, T = q.shape[:2]
    vh = v.reshape(b, T, DIFF_HEADS, 2 * DIFF_HEAD_DIM)
    vch = v_c.reshape(b, v_c.shape[1], DIFF_HEADS, 2 * DIFF_HEAD_DIM)
    lp = lam_p.astype(jnp.float32)
    lam = jnp.exp(jnp.sum(lp[0] * lp[1])) - jnp.exp(jnp.sum(lp[2] * lp[3])) + lam_init
    k_all = jnp.concatenate([kcn, kn], axis=1)
    v_all = jnp.concatenate([vch, vh], axis=1)
    nb = T // Q_BLOCK
    qb = jnp.swapaxes(qn.reshape(b, nb, Q_BLOCK, DIFF_HEADS, 2, DIFF_HEAD_DIM), 0, 1)
    o = lax.map(lambda blk: diff_attend(blk, k_all, v_all, lam), qb)
    o = jnp.swapaxes(o, 0, 1).reshape(b, T, DIFF_HEADS, 2 * DIFF_HEAD_DIM)
    oc = diff_attend(qcn, kcn, vch, lam)

    def finish(o, z):
        o = rms_norm(o, subln_g) * (1.0 - lam_init)
        return o.reshape(o.shape[0], o.shape[1], BRANCH_WIDTH) * jax.nn.silu(z)

    return finish(o, z), finish(oc, z_c)


def mlstm_scan(q, k, v, log_i, log_f, state):
    b, T, H, dk = q.shape
    dv = v.shape[-1]
    L = ML_CHUNK
    nc = T // L
    q = q.reshape(b, nc, L, H, dk)
    k = (k * dk ** -0.5).reshape(b, nc, L, H, dk)
    v = v.reshape(b, nc, L, H, dv)
    li = jnp.moveaxis(log_i.reshape(b, nc, L, H), 2, -1)
    lf = jnp.moveaxis(log_f.reshape(b, nc, L, H), 2, -1)
    bcum = jnp.cumsum(lf, axis=-1)
    b_last = bcum[..., -1]
    g = b_last[..., None] - bcum + li
    m_loc = jnp.max(g, axis=-1)
    w = jnp.exp(g - m_loc[..., None])
    C_loc = jnp.einsum('bchl,bclhk,bclhv->bchkv', w, k, v)
    n_loc = jnp.einsum('bchl,bclhk->bchk', w, k)

    def step(carry, inp):
        C, n, m = carry
        Cl, nl, ml, bl = inp
        m_new = jnp.maximum(bl + m, ml)
        a = jnp.exp(bl + m - m_new)
        s = jnp.exp(ml - m_new)
        return (a[..., None, None] * C + s[..., None, None] * Cl,
                a[..., None] * n + s[..., None] * nl, m_new), (C, n, m)

    final, (Cp, npv, mp) = lax.scan(step, state, (jnp.moveaxis(C_loc, 1, 0), jnp.moveaxis(n_loc, 1, 0),
                                                  jnp.moveaxis(m_loc, 1, 0), jnp.moveaxis(b_last, 1, 0)))
    Cp = jnp.moveaxis(Cp, 0, 1)
    npv = jnp.moveaxis(npv, 0, 1)
    mp = jnp.moveaxis(mp, 0, 1)
    tri = jnp.tril(jnp.ones((L, L), dtype=bool))
    Dm = jnp.where(tri, bcum[..., :, None] - bcum[..., None, :] + li[..., None, :], -jnp.inf)
    e = bcum + mp[..., None]
    m_t = jnp.maximum(e, jnp.max(Dm, axis=-1))
    Wts = jnp.exp(Dm - m_t[..., None]) * jnp.einsum('bclhk,bcshk->bchls', q, k)
    sc = jnp.exp(e - m_t)
    num = (jnp.einsum('bchls,bcshv->bclhv', Wts, v)
           + jnp.einsum('bclhk,bchkv->bclhv', q, Cp) * jnp.swapaxes(sc, -1, -2)[..., None])
    den = jnp.sum(Wts, axis=-1) + jnp.einsum('bclhk,bchk->bchl', q, npv) * sc
    den = jnp.maximum(jnp.abs(den), jnp.exp(-m_t))
    h = num / jnp.swapaxes(den, -1, -2)[..., None]
    return h.reshape(b, T, H, dv).astype(v.dtype), final


def mlstm_branch(qk, v, o, z, ig, fg, qk_c, v_c, o_c, z_c, ig_c, fg_c,
                 conv_w, conv_b, i_bias, f_bias, norm_g):
    def prep(qk, v, ig, fg):
        b, T, _ = qk.shape
        u = jax.nn.silu(dwconv(qk, conv_w, conv_b))
        q, k = jnp.split(u, 2, axis=-1)
        log_i = ig.astype(jnp.float32).reshape(b, T, 2, ML_HEADS) + i_bias.astype(jnp.float32)
        log_f = jax.nn.log_sigmoid(fg.astype(jnp.float32).reshape(b, T, 2, ML_HEADS)
                                   + f_bias.astype(jnp.float32))
        return (q.reshape(b, T, ML_HEADS, ML_QK_DIM), k.reshape(b, T, ML_HEADS, ML_QK_DIM),
                v.reshape(b, T, ML_HEADS, ML_V_DIM), log_i, log_f)

    q, k, vh, li, lf = prep(qk, v, ig, fg)
    qc, kc, vc, lic, lfc = prep(qk_c, v_c, ig_c, fg_c)
    b = q.shape[0]
    s0 = (jnp.zeros((b, ML_HEADS, ML_QK_DIM, ML_V_DIM), jnp.float32),
          jnp.zeros((b, ML_HEADS, ML_QK_DIM), jnp.float32),
          jnp.zeros((b, ML_HEADS), jnp.float32))
    hc_f, st_f = mlstm_scan(qc, kc, vc, lic[:, :, 0], lfc[:, :, 0], s0)
    hc_b, st_b = mlstm_scan(flip(qc), flip(kc), flip(vc), flip(lic[:, :, 1]), flip(lfc[:, :, 1]), s0)
    h_f, _ = mlstm_scan(q, k, vh, li[:, :, 0], lf[:, :, 0], st_f)
    h_b, _ = mlstm_scan(flip(q), flip(k), flip(vh), flip(li[:, :, 1]), flip(lf[:, :, 1]), st_b)

    def finish(h_f, h_b_rev, o, z):
        bb, T = o.shape[:2]
        h = (h_f + flip(h_b_rev)) * jax.nn.sigmoid(o).reshape(bb, T, ML_HEADS, ML_V_DIM)
        h = rms_norm(h, norm_g.reshape(ML_HEADS, ML_V_DIM)).reshape(bb, T, BRANCH_WIDTH)
        return h * jax.nn.silu(z)

    return finish(h_f, h_b, o, z), finish(hc_f, hc_b, o_c, z_c)


def layer_fwd(x, xc, c, c_ctx, cos, sin, lam_init, w_mod, b_mod, norm_g, w_in,
              ssd_conv_w, ssd_conv_b, ssd_a_log, ssd_dt_bias, ssd_d, ssd_norm_g,
              diff_qn_g, diff_kn_g, diff_lambda, diff_subln_g,
              ml_conv_w, ml_conv_b, ml_i_bias, ml_f_bias, ml_norm_g, w_branch, w_out):
    shift, scale, gate = jnp.split(jax.nn.silu(c) @ w_mod + b_mod, 3, axis=-1)
    shift_c, scale_c, gate_c = jnp.split(jax.nn.silu(c_ctx) @ w_mod + b_mod, 3, axis=-1)
    h = rms_norm(x, norm_g) * (1.0 + scale[:, None]) + shift[:, None]
    hc = rms_norm(xc, norm_g) * (1.0 + scale_c) + shift_c
    w_parts = split_cols(w_in, IN_SIZES)
    p = [h @ w for w in w_parts]
    pc = [hc @ w for w in w_parts]
    y_a, yc_a = ssd_branch(p[0], p[1], p[2], pc[0], pc[1], pc[2], ssd_conv_w, ssd_conv_b,
                           ssd_a_log, ssd_dt_bias, ssd_d, ssd_norm_g)
    y_b, yc_b = diff_branch(p[3], p[4], p[5], p[6], pc[3], pc[4], pc[5], pc[6],
                            diff_qn_g, diff_kn_g, diff_lambda, diff_subln_g, lam_init, cos, sin)
    y_c, yc_c = mlstm_branch(p[7], p[8], p[9], p[10], p[11], p[12],
                             pc[7], pc[8], pc[9], pc[10], pc[11], pc[12],
                             ml_conv_w, ml_conv_b, ml_i_bias, ml_f_bias, ml_norm_g)

    def merge(ya, yb, yc, gates):
        g = jax.nn.sigmoid(gates).reshape(gates.shape[:-1] + (N_BRANCH, D_MODEL))
        mixed = (g[..., 0, :] * (ya @ w_branch[0]) + g[..., 1, :] * (yb @ w_branch[1])
                 + g[..., 2, :] * (yc @ w_branch[2]))
        return mixed @ w_out

    x_new = x + gate[:, None] * merge(y_a, y_b, y_c, p[13])
    xc_new = xc + gate_c * merge(yc_a, yc_b, yc_c, pc[13])
    return xc_new, x_new


def setup_inputs(seed: int = 0) -> dict:
    key = jax.random.key(seed)
    ks = jax.random.split(key, 28)
    f32 = jnp.float32

    def nrm(k, shape, s):
        return jax.random.normal(k, shape, f32) * s

    dt0 = jnp.exp(jax.random.uniform(ks[11], (DEPTH, 2, SSD_HEADS), f32,
                                     minval=math.log(1e-3), maxval=math.log(1e-1)))
    return {
        'x': nrm(ks[0], (BATCH, SEQ, D_MODEL), 1.0),
        'c': nrm(ks[1], (BATCH, D_MODEL), 1.0),
        'ctx': nrm(ks[2], (BATCH, CTX_LEN, D_MODEL), 1.0),
        'c_ctx': nrm(ks[3], (D_MODEL,), 1.0),
        'w_mod': nrm(ks[4], (DEPTH, D_MODEL, 3 * D_MODEL), 0.3 * D_MODEL ** -0.5),
        'b_mod': nrm(ks[5], (DEPTH, 3 * D_MODEL), 0.02),
        'norm_g': 1.0 + nrm(ks[6], (DEPTH, D_MODEL), 0.02),
        'w_in': nrm(ks[7], (DEPTH, D_MODEL, D_IN), D_MODEL ** -0.5),
        'ssd_conv_w': nrm(ks[8], (DEPTH, CONV_K, SSD_XBC), CONV_K ** -0.5),
        'ssd_conv_b': nrm(ks[9], (DEPTH, SSD_XBC), 0.02),
        'ssd_a_log': jnp.log(jax.random.uniform(ks[10], (DEPTH, 2, SSD_HEADS), f32, minval=1.0, maxval=16.0)),
        'ssd_dt_bias': dt0 + jnp.log(-jnp.expm1(-dt0)),
        'ssd_d': 1.0 + nrm(ks[12], (DEPTH, SSD_HEADS), 0.02),
        'ssd_norm_g': 1.0 + nrm(ks[13], (DEPTH, BRANCH_WIDTH), 0.02),
        'diff_qn_g': 1.0 + nrm(ks[14], (DEPTH, DIFF_HEAD_DIM), 0.02),
        'diff_kn_g': 1.0 + nrm(ks[15], (DEPTH, DIFF_HEAD_DIM), 0.02),
        'diff_lambda': nrm(ks[16], (DEPTH, 4, DIFF_HEAD_DIM), 0.1),
        'diff_subln_g': 1.0 + nrm(ks[17], (DEPTH, 2 * DIFF_HEAD_DIM), 0.02),
        'ml_conv_w': nrm(ks[18], (DEPTH, CONV_K, ML_QK), CONV_K ** -0.5),
        'ml_conv_b': nrm(ks[19], (DEPTH, ML_QK), 0.02),
        'ml_i_bias': nrm(ks[20], (DEPTH, 2, ML_HEADS), 0.1),
        'ml_f_bias': jnp.linspace(3.0, 6.0, ML_HEADS, dtype=f32) + nrm(ks[21], (DEPTH, 2, ML_HEADS), 0.1),
        'ml_norm_g': 1.0 + nrm(ks[22], (DEPTH, BRANCH_WIDTH), 0.02),
        'w_branch': nrm(ks[23], (DEPTH, N_BRANCH, BRANCH_WIDTH, D_MODEL), BRANCH_WIDTH ** -0.5),
        'w_out': nrm(ks[24], (DEPTH, D_MODEL, D_MODEL), D_MODEL ** -0.5),
    }


def reference(x, c, ctx, c_ctx, w_mod, b_mod, norm_g, w_in, ssd_conv_w, ssd_conv_b, ssd_a_log,
              ssd_dt_bias, ssd_d, ssd_norm_g, diff_qn_g, diff_kn_g, diff_lambda, diff_subln_g,
              ml_conv_w, ml_conv_b, ml_i_bias, ml_f_bias, ml_norm_g, w_branch, w_out):
    cos, sin = axial_rope_tables(x.shape[1])
    xc = ctx
    for l in range(DEPTH):
        lam_init = 0.8 - 0.6 * math.exp(-0.3 * l)
        xc, x = layer_fwd(x, xc, c, c_ctx, cos, sin, lam_init, w_mod[l], b_mod[l], norm_g[l], w_in[l],
                          ssd_conv_w[l], ssd_conv_b[l], ssd_a_log[l], ssd_dt_bias[l], ssd_d[l], ssd_norm_g[l],
                          diff_qn_g[l], diff_kn_g[l], diff_lambda[l], diff_subln_g[l],
                          ml_conv_w[l], ml_conv_b[l], ml_i_bias[l], ml_f_bias[l], ml_norm_g[l],
                          w_branch[l], w_out[l])
    return x
```

```python
import functools
import math

import numpy as np
import jax
import jax.numpy as jnp
from jax import lax
from jax.experimental import pallas as pl
from jax.experimental.pallas import tpu as pltpu

F32 = jnp.float32
BF16 = jnp.bfloat16

GRID_W = 64
EPS = 1e-6
CONV_K = 3
SSD_HEADS = 16
SSD_HEAD_DIM = 64
SSD_GROUPS = 2
SSD_STATE = 128
DIFF_HEADS = 8
DIFF_HEAD_DIM = 64
ROPE_BASE = 10000.0
ML_HEADS = 4
ML_QK_DIM = 128
ML_V_DIM = 256
CHUNK = 128
LANES = 128
BF16_ROWS = 16

OFF_ZS, OFF_Q, OFF_K, OFF_V, OFF_ZD = 0, 1024, 2048, 3072, 4096
OFF_MQK, OFF_MV, OFF_MO, OFF_MZ, OFF_GATES, OFF_XBC = 5120, 6144, 7168, 8192, 9216, 12288
N_BIG = 13824
SM_IG, SM_FG = 32, 40

VMEM_LIMIT = 56 * 1024 * 1024


def _cparams(sem):
    return pltpu.CompilerParams(dimension_semantics=sem, vmem_limit_bytes=VMEM_LIMIT)


def _split3(v):
    h = v.astype(BF16)
    r = v - h.astype(F32)
    m = r.astype(BF16)
    l = (r - m.astype(F32)).astype(BF16)
    return h, m, l


def _dot(a, b):
    return jnp.dot(a, b, preferred_element_type=F32)


def _dot_nt(a, b):
    return lax.dot_general(a, b, (((1,), (1,)), ((), ())), preferred_element_type=F32)


def _dot_tn(a, b):
    return lax.dot_general(a, b, (((0,), (0,)), ((), ())), preferred_element_type=F32)


def _dot_exact_rhs01(v, m01):
    h, m, l = _split3(v)
    return _dot(h, m01) + _dot(m, m01) + _dot(l, m01)


def _dot_exact_lhs01(m01, v):
    h, m, l = _split3(v)
    return _dot(m01, h) + _dot(m01, m) + _dot(m01, l)


def _sigmoid(x):
    return 1.0 / (1.0 + jnp.exp(-x))


def _silu(x):
    return x * _sigmoid(x)


def _softplus(x):
    return jnp.maximum(x, 0.0) + jnp.log(1.0 + jnp.exp(-jnp.abs(x)))


def _mod_kernel(cc_ref, w_ref, b_ref, o_ref):
    a = _silu(cc_ref[...])
    w = w_ref[0]
    ah, am, _ = _split3(a)
    wh = w.astype(BF16)
    wm = (w - wh.astype(F32)).astype(BF16)
    o_ref[0] = _dot(ah, wh) + _dot(am, wh) + _dot(ah, wm) + b_ref[0]


def _modulation(cc, w_mod, b_mod):
    depth, d, d3 = w_mod.shape
    tn = 1024
    return pl.pallas_call(
        _mod_kernel,
        grid=(depth, d3 // tn),
        in_specs=[pl.BlockSpec((8, d), lambda l, j: (0, 0)),
                  pl.BlockSpec((1, d, tn), lambda l, j: (l, 0, j)),
                  pl.BlockSpec((1, 1, tn), lambda l, j: (l, 0, j))],
        out_specs=pl.BlockSpec((1, 8, tn), lambda l, j: (l, 0, j)),
        out_shape=jax.ShapeDtypeStruct((depth, 8, d3), F32),
        compiler_params=_cparams(("arbitrary", "arbitrary")),
        name="modulation",
    )(cc, w_mod, b_mod.reshape(depth, 1, d3))


def _inproj_kernel(x_ref, mod_ref, g_ref, w_ref, ws_ref, o_ref, os_ref, h_ref, *, ctx_len, ctx_row):
    b, i, j = pl.program_id(0), pl.program_id(1), pl.program_id(2)
    tm, d = h_ref.shape

    @pl.when(j == 0)
    def _():
        x = x_ref[0]
        ms = jnp.mean(x * x, axis=-1, keepdims=True)
        y = x * lax.rsqrt(ms + EPS) * g_ref[...]
        row = i * tm + lax.broadcasted_iota(jnp.int32, (tm, 1), 0)
        is_ctx = row < ctx_len
        mlat = mod_ref[pl.ds(b, 1), :]
        mctx = mod_ref[pl.ds(ctx_row, 1), :]
        shift = jnp.where(is_ctx, mctx[:, :d], mlat[:, :d])
        scale = jnp.where(is_ctx, mctx[:, d:2 * d], mlat[:, d:2 * d])
        h = (y * (1.0 + scale) + shift).astype(BF16)
        h_ref[...] = h
        os_ref[0] = _dot(h, ws_ref[...])

    o_ref[0] = _dot(h_ref[...], w_ref[...]).astype(BF16)


def _row_tile(ta, target):
    best = BF16_ROWS
    for t in range(BF16_ROWS, target + 1, BF16_ROWS):
        if ta % t == 0:
            best = t
    return best


def _inproj(xall, mod_l, g, w_big, w_small, ctx_len):
    nb, ta, d = xall.shape
    tm = _row_tile(ta, 1056)
    tn = 1536
    kern = functools.partial(_inproj_kernel, ctx_len=ctx_len, ctx_row=nb)
    return pl.pallas_call(
        kern,
        grid=(nb, ta // tm, N_BIG // tn),
        in_specs=[pl.BlockSpec((1, tm, d), lambda b, i, j: (b, i, 0)),
                  pl.BlockSpec(mod_l.shape, lambda b, i, j: (0, 0)),
                  pl.BlockSpec((1, d), lambda b, i, j: (0, 0)),
                  pl.BlockSpec((d, tn), lambda b, i, j: (0, j)),
                  pl.BlockSpec((d, LANES), lambda b, i, j: (0, 0))],
        out_specs=[pl.BlockSpec((1, tm, tn), lambda b, i, j: (b, i, j)),
                   pl.BlockSpec((1, tm, LANES), lambda b, i, j: (b, i, 0))],
        out_shape=[jax.ShapeDtypeStruct((nb, ta, N_BIG), BF16),
                   jax.ShapeDtypeStruct((nb, ta, LANES), F32)],
        scratch_shapes=[pltpu.VMEM((tm, d), BF16)],
        compiler_params=_cparams(("arbitrary", "arbitrary", "arbitrary")),
        name="inproj",
    )(xall, mod_l, g, w_big, w_small)


def _bwd_chunk(i, ncc, nch):
    return jnp.where(i < ncc, ncc - 1 - i, nch - 1 + ncc - i)


def _conv_silu(x_ref, xp_ref, xn_ref, cw_ref, cb_ref, c, ncc, nch):
    x = x_ref[0].astype(F32)
    L = x.shape[0]
    has_prev = jnp.logical_and(c != 0, c != ncc)
    has_next = jnp.logical_and(c != ncc - 1, c != nch - 1)
    prow = jnp.where(has_prev, xp_ref[0][BF16_ROWS - 1:BF16_ROWS, :].astype(F32), 0.0)
    nrow = jnp.where(has_next, xn_ref[0][0:1, :].astype(F32), 0.0)
    rid = lax.broadcasted_iota(jnp.int32, x.shape, 0)
    xprev = jnp.where(rid == 0, prow, pltpu.roll(x, 1, 0))
    xnext = jnp.where(rid == L - 1, nrow, pltpu.roll(x, L - 1, 0))
    u = xprev * cw_ref[0:1, :] + x * cw_ref[1:2, :] + xnext * cw_ref[2:3, :] + cb_ref[...]
    return _silu(u)


def _chunk_specs(width, col_block, ncc, nch, nrow16):
    per = CHUNK // BF16_ROWS

    def mk(cfun):
        main = pl.BlockSpec((1, CHUNK, width), lambda b, i: (b, cfun(i), col_block))
        prev = pl.BlockSpec((1, BF16_ROWS, width),
                            lambda b, i: (b, jnp.maximum(cfun(i) * per - 1, 0), col_block))
        nxt = pl.BlockSpec((1, BF16_ROWS, width),
                           lambda b, i: (b, jnp.minimum((cfun(i) + 1) * per, nrow16 - 1), col_block))
        return [main, prev, nxt]

    return mk(lambda i: i) + mk(lambda i: _bwd_chunk(i, ncc, nch))


def _full_spec(a):
    nd = a.ndim
    return pl.BlockSpec(a.shape, lambda b, i: (0,) * nd)


def _ssd_direction(d, c, x_ref, xp_ref, xn_ref, ps_ref, cw_ref, cb_ref, dtb_ref, a_ref, dsk_ref,
                   tri_ref, trit_ref, e_ref, y_ref, s_ref, ncc, nch):
    L = CHUNK
    hp = SSD_HEADS * SSD_HEAD_DIM
    gn = SSD_GROUPS * SSD_STATE
    gw = hp // SSD_GROUPS
    u = _conv_silu(x_ref, xp_ref, xn_ref, cw_ref, cb_ref, c, ncc, nch)
    xs = u[:, :hp]
    bm = u[:, hp:hp + gn].astype(BF16)
    cm = u[:, hp + gn:].astype(BF16)

    dt = _softplus(ps_ref[0] + dtb_ref[...])
    a = dt * a_ref[...]
    tri = tri_ref[d]
    acs = _dot_exact_lhs01(tri, a)
    acs_t = _dot_exact_rhs01(a.T, trit_ref[d])
    last = L - 1 if d == 0 else 0
    acs_last = acs[last:last + 1, :]
    dend = jnp.exp(acs_last - acs)
    eacs = jnp.exp(acs)

    e01 = e_ref[d]
    dt_e = _dot_exact_rhs01(dt, e01)
    dtend_e = _dot_exact_rhs01(dt * dend, e01)
    eacs_e = _dot_exact_rhs01(eacs, e01)
    xd = xs * dt_e
    xde = (xs * dtend_e).astype(BF16)
    cd_e = eacs_e[last:last + 1, :]

    mask = tri > 0
    lane = lax.broadcasted_iota(jnp.int32, (L, LANES), 1)
    left = lane < SSD_HEAD_DIM
    heads_per_group = SSD_HEADS // SSD_GROUPS
    ys = []
    for g in range(SSD_GROUPS):
        cg = cm[:, g * SSD_STATE:(g + 1) * SSD_STATE]
        bg = bm[:, g * SSD_STATE:(g + 1) * SSD_STATE]
        sg = _dot_nt(cg, bg)
        s_prev = s_ref[d, g]
        y_off = _dot(cg, s_prev.astype(BF16)) * eacs_e[:, g * gw:(g + 1) * gw]
        pairs = []
        for k in range(heads_per_group // 2):
            h0 = g * heads_per_group + 2 * k
            ms = []
            for h in (h0, h0 + 1):
                j = SSD_HEADS * d + h
                seg = acs[:, j:j + 1] - acs_t[j:j + 1, :]
                ms.append((sg * jnp.exp(jnp.where(mask, seg, -jnp.inf))).astype(BF16))
            xp = xd[:, h0 * SSD_HEAD_DIM:(h0 + 2) * SSD_HEAD_DIM]
            rhs = jnp.concatenate([jnp.where(left, xp, 0.0), jnp.where(left, 0.0, xp)], axis=0).astype(BF16)
            pairs.append(_dot(jnp.concatenate(ms, axis=1), rhs))
        ys.append(jnp.concatenate(pairs, axis=1) + y_off)
        s_ref[d, g] = s_prev * cd_e[:, g * gw:(g + 1) * gw] + _dot_tn(bg, xde[:, g * gw:(g + 1) * gw])
    y = jnp.concatenate(ys, axis=1)
    if d == 0:
        y = y + dsk_ref[...] * xs
    y_ref[0, 0] = y.astype(BF16)


def _ssd_kernel(xf, xfp, xfn, xb, xbp, xbn, psf, psb, cw, cb, dtb, aneg, dsk, tri, trit, e01,
                yf, yb, s_ref, *, ncc, nch):
    i = pl.program_id(1)

    @pl.when(i == 0)
    def _():
        s_ref[...] = jnp.zeros_like(s_ref)

    _ssd_direction(0, i, xf, xfp, xfn, psf, cw, cb, dtb, aneg, dsk, tri, trit, e01, yf, s_ref, ncc, nch)
    _ssd_direction(1, _bwd_chunk(i, ncc, nch), xb, xbp, xbn, psb, cw, cb, dtb, aneg, dsk, tri, trit, e01,
                   yb, s_ref, ncc, nch)


def _ssd(p_big, p_small, cw, cb, dtb, aneg, dsk, tri, trit, e01, ctx_len):
    nb, ta, _ = p_big.shape
    ncc, nch = ctx_len // CHUNK, ta // CHUNK
    hp = SSD_HEADS * SSD_HEAD_DIM
    width = hp + 2 * SSD_GROUPS * SSD_STATE
    xspecs = _chunk_specs(width, OFF_XBC // width, ncc, nch, ta // BF16_ROWS)
    ps_f = pl.BlockSpec((1, CHUNK, LANES), lambda b, i: (b, i, 0))
    ps_b = pl.BlockSpec((1, CHUNK, LANES), lambda b, i: (b, _bwd_chunk(i, ncc, nch), 0))
    consts = [cw, cb, dtb, aneg, dsk, tri, trit, e01]
    kern = functools.partial(_ssd_kernel, ncc=ncc, nch=nch)
    yf, yb = pl.pallas_call(
        kern,
        grid=(nb, nch),
        in_specs=xspecs + [ps_f, ps_b] + [_full_spec(a) for a in consts],
        out_specs=[pl.BlockSpec((1, 1, CHUNK, hp), lambda b, i: (0, b, i, 0)),
                   pl.BlockSpec((1, 1, CHUNK, hp), lambda b, i: (0, b, _bwd_chunk(i, ncc, nch), 0))],
        out_shape=[jax.ShapeDtypeStruct((1, nb, ta, hp), BF16)] * 2,
        scratch_shapes=[pltpu.VMEM((2, SSD_GROUPS, SSD_STATE, hp // SSD_GROUPS), F32)],
        compiler_params=_cparams(("arbitrary", "arbitrary")),
        name="ssd_scan",
    )(p_big, p_big, p_big, p_big, p_big, p_big, p_small, p_small, *consts)
    return yf[0], yb[0]


def _ml_direction(d, c, qk_ref, qkp_ref, qkn_ref, v_ref, ps_ref, cw_ref, cb_ref, ib_ref, fb_ref,
                  tri_ref, trit_ref, h_ref, c_ref, n_ref, m_ref, ncc, nch):
    L = CHUNK
    hq = ML_HEADS * ML_QK_DIM
    u = _conv_silu(qk_ref, qkp_ref, qkn_ref, cw_ref, cb_ref, c, ncc, nch)
    ps = ps_ref[0]
    li = ps + ib_ref[...]
    lf = -_softplus(-(ps + fb_ref[...]))
    tri = tri_ref[d]
    trit = trit_ref[d]
    bcum = _dot_exact_lhs01(tri, lf)
    bcum_t = _dot_exact_rhs01(lf.T, trit)
    li_t = li.T
    mask = tri > 0
    last = L - 1 if d == 0 else 0
    outs = []
    for h in range(ML_HEADS):
        ji = SM_IG + ML_HEADS * d + h
        jf = SM_FG + ML_HEADS * d + h
        q = u[:, h * ML_QK_DIM:(h + 1) * ML_QK_DIM]
        k = u[:, hq + h * ML_QK_DIM:hq + (h + 1) * ML_QK_DIM] * (ML_QK_DIM ** -0.5)
        v = v_ref[0][:, h * ML_V_DIM:(h + 1) * ML_V_DIM]
        qb = q.astype(BF16)
        kb = k.astype(BF16)
        bc_col = bcum[:, jf:jf + 1]
        bc_row = bcum_t[jf:jf + 1, :]
        li_col = li[:, ji:ji + 1]
        li_row = li_t[ji:ji + 1, :]
        b_last = bc_col[last:last + 1, :]
        c_prev = c_ref[d, h]
        n_prev = n_ref[d, h]
        m_prev = m_ref[d, h][:, 0:1]

        dm = jnp.where(mask, bc_col - bc_row + li_row, -jnp.inf)
        e = bc_col + m_prev
        m_t = jnp.maximum(e, jnp.max(dm, axis=-1, keepdims=True))
        wts = jnp.exp(dm - m_t) * _dot_nt(qb, kb)
        sc = jnp.exp(e - m_t)
        num = _dot(wts.astype(BF16), v) + _dot(qb, c_prev.astype(BF16)) * sc
        den = jnp.sum(wts, axis=-1, keepdims=True) + jnp.sum(q * n_prev, axis=-1, keepdims=True) * sc
        den = jnp.maximum(jnp.abs(den), jnp.exp(-m_t))
        outs.append(num / den)

        gl = b_last - bc_col + li_col
        m_loc = jnp.max(gl, axis=0, keepdims=True)
        w = jnp.exp(gl - m_loc)
        kw = k * w
        c_loc = _dot_tn(kw.astype(BF16), v)
        n_loc = jnp.sum(kw, axis=0, keepdims=True)
        m_new = jnp.maximum(b_last + m_prev, m_loc)
        a_s = jnp.exp(b_last + m_prev - m_new)
        s_s = jnp.exp(m_loc - m_new)
        c_ref[d, h] = a_s * c_prev + s_s * c_loc
        n_ref[d, h] = a_s * n_prev + s_s * n_loc
        m_ref[d, h] = jnp.broadcast_to(m_new, (1, LANES))
    h_ref[0, 0] = jnp.concatenate(outs, axis=1).astype(BF16)


def _ml_kernel(qf, qfp, qfn, qb, qbp, qbn, vf, vb, psf, psb, cw, cb, ib, fb, tri, trit,
               hf, hb, c_ref, n_ref, m_ref, *, ncc, nch):
    i = pl.program_id(1)

    @pl.when(i == 0)
    def _():
        c_ref[...] = jnp.zeros_like(c_ref)
        n_ref[...] = jnp.zeros_like(n_ref)
        m_ref[...] = jnp.zeros_like(m_ref)

    _ml_direction(0, i, qf, qfp, qfn, vf, psf, cw, cb, ib, fb, tri, trit, hf, c_ref, n_ref, m_ref, ncc, nch)
    _ml_direction(1, _bwd_chunk(i, ncc, nch), qb, qbp, qbn, vb, psb, cw, cb, ib, fb, tri, trit, hb,
                  c_ref, n_ref, m_ref, ncc, nch)


def _mlstm(p_big, p_small, cw, cb, ib, fb, tri, trit, ctx_len):
    nb, ta, _ = p_big.shape
    ncc, nch = ctx_len // CHUNK, ta // CHUNK
    wq = 2 * ML_HEADS * ML_QK_DIM
    wv = ML_HEADS * ML_V_DIM
    qspecs = _chunk_specs(wq, OFF_MQK // wq, ncc, nch, ta // BF16_ROWS)
    v_f = pl.BlockSpec((1, CHUNK, wv), lambda b, i: (b, i, OFF_MV // wv))
    v_b = pl.BlockSpec((1, CHUNK, wv), lambda b, i: (b, _bwd_chunk(i, ncc, nch), OFF_MV // wv))
    ps_f = pl.BlockSpec((1, CHUNK, LANES), lambda b, i: (b, i, 0))
    ps_b = pl.BlockSpec((1, CHUNK, LANES), lambda b, i: (b, _bwd_chunk(i, ncc, nch), 0))
    consts = [cw, cb, ib, fb, tri, trit]
    kern = functools.partial(_ml_kernel, ncc=ncc, nch=nch)
    hf, hb = pl.pallas_call(
        kern,
        grid=(nb, nch),
        in_specs=qspecs + [v_f, v_b, ps_f, ps_b] + [_full_spec(a) for a in consts],
        out_specs=[pl.BlockSpec((1, 1, CHUNK, wv), lambda b, i: (0, b, i, 0)),
                   pl.BlockSpec((1, 1, CHUNK, wv), lambda b, i: (0, b, _bwd_chunk(i, ncc, nch), 0))],
        out_shape=[jax.ShapeDtypeStruct((1, nb, ta, wv), BF16)] * 2,
        scratch_shapes=[pltpu.VMEM((2, ML_HEADS, ML_QK_DIM, ML_V_DIM), F32),
                        pltpu.VMEM((2, ML_HEADS, 1, ML_QK_DIM), F32),
                        pltpu.VMEM((2, ML_HEADS, 1, LANES), F32)],
        compiler_params=_cparams(("arbitrary", "arbitrary")),
        name="mlstm_scan",
    )(p_big, p_big, p_big, p_big, p_big, p_big, p_big, p_big, p_small, p_small, *consts)
    return hf[0], hb[0]


def _qk_prep_kernel(q_ref, k_ref, cos_ref, sa_ref, sb_ref, gq_ref, gk_ref, gm_ref, qo_ref, ko_ref, *, qscale):
    cos, sa, sb = cos_ref[...], sa_ref[...], sb_ref[...]
    gm = gm_ref[...]
    nheads = q_ref.shape[2] // LANES

    def prep(t, g, mult):
        t = t.astype(F32)
        sq = t * t
        sh = sq.astype(BF16)
        sm = (sq - sh.astype(F32)).astype(BF16)
        ms = (_dot(sh, gm) + _dot(sm, gm)) * (1.0 / DIFF_HEAD_DIM)
        y = t * lax.rsqrt(ms + EPS) * g
        r = y * cos + pltpu.roll(y, LANES - 16, 1) * sa + pltpu.roll(y, 16, 1) * sb
        return (r * mult).astype(BF16)

    for h in range(nheads):
        sl = slice(h * LANES, (h + 1) * LANES)
        qo_ref[0, :, sl] = prep(q_ref[0, :, sl], gq_ref[...], qscale)
        ko_ref[0, :, sl] = prep(k_ref[0, :, sl], gk_ref[...], 1.0)


def _qk_prep(p_big, cos, sa, sb, gq, gk, gm):
    nb, ta, _ = p_big.shape
    w = DIFF_HEADS * 2 * DIFF_HEAD_DIM
    tm = _row_tile(ta, 1056)
    qscale = DIFF_HEAD_DIM ** -0.5 * math.log2(math.e)
    row = lambda b, i: (i, 0)
    return pl.pallas_call(
        functools.partial(_qk_prep_kernel, qscale=qscale),
        grid=(nb, ta // tm),
        in_specs=[pl.BlockSpec((1, tm, w), lambda b, i: (b, i, OFF_Q // w)),
                  pl.BlockSpec((1, tm, w), lambda b, i: (b, i, OFF_K // w)),
                  pl.BlockSpec((tm, LANES), row), pl.BlockSpec((tm, LANES), row), pl.BlockSpec((tm, LANES), row),
                  _full_spec(gq), _full_spec(gk), _full_spec(gm)],
        out_specs=[pl.BlockSpec((1, tm, w), lambda b, i: (b, i, 0))] * 2,
        out_shape=[jax.ShapeDtypeStruct((nb, ta, w), BF16)] * 2,
        compiler_params=_cparams(("arbitrary", "arbitrary")),
        name="qk_prep",
    )(p_big, p_big, cos, sa, sb, gq, gk, gm)


def _attn_kernel(q_ref, k_ref, v_ref, z_ref, lam_ref, gq_ref, gk_ref, sg_ref, o_ref, vp_ref,
                 *, tk, ctx_len, lam_init):
    qi = pl.program_id(2)
    tq = q_ref.shape[1]
    ta = k_ref.shape[1]
    dh = DIFF_HEAD_DIM

    @pl.when(qi == 0)
    def _():
        vp_ref[:, :LANES] = v_ref[0]
        vp_ref[:, LANES:] = jnp.ones((ta, LANES), BF16)

    lp = lam_ref[...]
    lam = (jnp.exp(jnp.sum(lp[0:1] * lp[1:2], axis=-1, keepdims=True))
           - jnp.exp(jnp.sum(lp[2:3] * lp[3:4], axis=-1, keepdims=True)) + lam_init)
    shift = (jnp.max(jnp.abs(gq_ref[...]), axis=-1, keepdims=True)
             * jnp.max(jnp.abs(gk_ref[...]), axis=-1, keepdims=True)
             * (dh * dh ** -0.5 * math.log2(math.e)))

    q = q_ref[0]
    lane = lax.broadcasted_iota(jnp.int32, q.shape, 1)
    zero = jnp.zeros_like(q)
    q0 = jnp.where(lane < dh, q, zero)
    q1 = jnp.where(lane < dh, zero, q)
    nk = jnp.where(qi == 0, ctx_len // tk, ta // tk)

    def body(kc, carry):
        a0, a1 = carry
        off = pl.multiple_of(kc * tk, tk)
        k = k_ref[0, pl.ds(off, tk), :]
        vp = vp_ref[pl.ds(off, tk), :]
        e0 = jnp.exp2(_dot_nt(q0, k) - shift).astype(BF16)
        e1 = jnp.exp2(_dot_nt(q1, k) - shift).astype(BF16)
        return a0 + _dot(e0, vp), a1 + _dot(e1, vp)

    z0 = jnp.zeros((tq, 2 * LANES), F32)
    a0, a1 = lax.fori_loop(0, nk, body, (z0, z0))
    o = a0[:, :LANES] / a0[:, LANES:] - lam * (a1[:, :LANES] / a1[:, LANES:])
    ms = jnp.mean(o * o, axis=-1, keepdims=True)
    o = o * lax.rsqrt(ms + EPS) * sg_ref[...] * (1.0 - lam_init)
    o_ref[0] = (o * _silu(z_ref[0].astype(F32))).astype(BF16)


def _attention(qn, kn, p_big, lam_p, gq, gk, sg, ctx_len, lam_init):
    nb, ta, w = qn.shape
    tq = ctx_len
    tk = 256
    kern = functools.partial(_attn_kernel, tk=tk, ctx_len=ctx_len, lam_init=lam_init)
    full = lambda a: pl.BlockSpec(a.shape, lambda b, h, i: (0,) * a.ndim)
    return pl.pallas_call(
        kern,
        grid=(nb, DIFF_HEADS, ta // tq),
        in_specs=[pl.BlockSpec((1, tq, LANES), lambda b, h, i: (b, i, h)),
                  pl.BlockSpec((1, ta, LANES), lambda b, h, i: (b, 0, h)),
                  pl.BlockSpec((1, ta, LANES), lambda b, h, i: (b, 0, OFF_V // LANES + h)),
                  pl.BlockSpec((1, tq, LANES), lambda b, h, i: (b, i, OFF_ZD // LANES + h)),
                  full(lam_p), full(gq), full(gk), full(sg)],
        out_specs=pl.BlockSpec((1, tq, LANES), lambda b, h, i: (b, i, h)),
        out_shape=jax.ShapeDtypeStruct((nb, ta, w), BF16),
        scratch_shapes=[pltpu.VMEM((ta, 2 * LANES), BF16)],
        compiler_params=_cparams(("arbitrary", "arbitrary", "arbitrary")),
        name="diff_attention",
    )(qn, kn, p_big, p_big, lam_p, gq, gk, sg)


def _merge_kernel(x_ref, mod_ref, yf_ref, yb_ref, zs_ref, yd_ref, hf_ref, hb_ref, mo_ref, mz_ref, gt_ref,
                  sng_ref, mng_ref, wb_ref, wo_ref, o_ref, *, ctx_len, ctx_row):
    b, i = pl.program_id(0), pl.program_id(1)
    tm, d = x_ref.shape[1], x_ref.shape[2]

    ya = (yf_ref[0].astype(F32) + yb_ref[0].astype(F32)) * _silu(zs_ref[0].astype(F32))
    ya = ya * lax.rsqrt(jnp.mean(ya * ya, axis=-1, keepdims=True) + EPS) * sng_ref[...]

    hm = (hf_ref[0].astype(F32) + hb_ref[0].astype(F32)) * _sigmoid(mo_ref[0].astype(F32))
    parts = []
    for h in range(ML_HEADS):
        t = hm[:, h * ML_V_DIM:(h + 1) * ML_V_DIM]
        parts.append(t * lax.rsqrt(jnp.mean(t * t, axis=-1, keepdims=True) + EPS))
    yc = jnp.concatenate(parts, axis=1) * mng_ref[...] * _silu(mz_ref[0].astype(F32))

    g = _sigmoid(gt_ref[0].astype(F32))
    mixed = (g[:, :d] * _dot(ya.astype(BF16), wb_ref[0])
             + g[:, d:2 * d] * _dot(yd_ref[0], wb_ref[1])
             + g[:, 2 * d:] * _dot(yc.astype(BF16), wb_ref[2]))
    out = _dot(mixed.astype(BF16), wo_ref[...])

    row = i * tm + lax.broadcasted_iota(jnp.int32, (tm, 1), 0)
    gate = jnp.where(row < ctx_len, mod_ref[pl.ds(ctx_row, 1), 2 * d:], mod_ref[pl.ds(b, 1), 2 * d:])
    o_ref[0] = x_ref[0] + gate * out


def _merge(xall, mod_l, yf, yb, p_big, yd, hf, hb, sng, mng, wb, wo, ctx_len):
    nb, ta, d = xall.shape
    tm = _row_tile(ta, 352)
    kern = functools.partial(_merge_kernel, ctx_len=ctx_len, ctx_row=nb)
    blk = lambda col: pl.BlockSpec((1, tm, d), lambda b, i: (b, i, col))
    full = lambda a: pl.BlockSpec(a.shape, lambda b, i: (0,) * a.ndim)
    return pl.pallas_call(
        kern,
        grid=(nb, ta // tm),
        in_specs=[blk(0), full(mod_l), blk(0), blk(0), blk(OFF_ZS // d), blk(0), blk(0), blk(0),
                  blk(OFF_MO // d), blk(OFF_MZ // d),
                  pl.BlockSpec((1, tm, 3 * d), lambda b, i: (b, i, OFF_GATES // (3 * d))),
                  full(sng), full(mng), full(wb), full(wo)],
        out_specs=blk(0),
        out_shape=jax.ShapeDtypeStruct((nb, ta, d), F32),
        compiler_params=_cparams(("arbitrary", "arbitrary")),
        name="merge",
    )(xall, mod_l, yf, yb, p_big, yd, hf, hb, p_big, p_big, p_big, sng, mng, wb, wo)


def _rope_tables(ctx_len, seq):
    rows = seq // GRID_W
    row = jnp.repeat(jnp.arange(rows, dtype=F32), GRID_W)
    col = jnp.tile(jnp.arange(GRID_W, dtype=F32), rows)
    half = DIFF_HEAD_DIM // 2
    inv_freq = ROPE_BASE ** (-jnp.arange(0, half, 2, dtype=F32) / half)
    ang_r = row[:, None] * inv_freq
    ang_c = col[:, None] * inv_freq
    cos = jnp.concatenate([jnp.cos(ang_r), jnp.cos(ang_r), jnp.cos(ang_c), jnp.cos(ang_c)], axis=-1)
    sin = jnp.concatenate([jnp.sin(ang_r), jnp.sin(ang_r), jnp.sin(ang_c), jnp.sin(ang_c)], axis=-1)
    cos = jnp.concatenate([jnp.ones((ctx_len, DIFF_HEAD_DIM), F32), cos], axis=0)
    sin = jnp.concatenate([jnp.zeros((ctx_len, DIFF_HEAD_DIM), F32), sin], axis=0)
    cos = jnp.tile(cos, (1, LANES // DIFF_HEAD_DIM))
    sin = jnp.tile(sin, (1, LANES // DIFF_HEAD_DIM))
    first = (jnp.arange(LANES) % (DIFF_HEAD_DIM // 2)) < DIFF_HEAD_DIM // 4
    return cos, jnp.where(first, -sin, 0.0), jnp.where(first, 0.0, sin)


def _scan_constants():
    idx = np.arange(CHUNK)
    tri_f = (idx[None, :] <= idx[:, None]).astype(np.float32)
    tri = np.stack([tri_f, tri_f.T])
    trit = np.stack([tri_f.T, tri_f])
    e01 = np.zeros((2, LANES, SSD_HEADS * SSD_HEAD_DIM), np.float32)
    for d in range(2):
        for h in range(SSD_HEADS):
            e01[d, SSD_HEADS * d + h, h * SSD_HEAD_DIM:(h + 1) * SSD_HEAD_DIM] = 1.0
    half = np.arange(LANES) // DIFF_HEAD_DIM
    gm = (half[:, None] == half[None, :]).astype(np.float32)
    return (jnp.asarray(tri, BF16), jnp.asarray(trit, BF16), jnp.asarray(e01, BF16), jnp.asarray(gm, BF16))


def _pad_row(v, offset):
    v = v.reshape(1, -1).astype(F32)
    return jnp.pad(v, ((0, 0), (offset, LANES - offset - v.shape[1])))


def kernel(x, c, ctx, c_ctx, w_mod, b_mod, norm_g, w_in, ssd_conv_w, ssd_conv_b, ssd_a_log, ssd_dt_bias, ssd_d,
           ssd_norm_g, diff_qn_g, diff_kn_g, diff_lambda, diff_subln_g, ml_conv_w, ml_conv_b, ml_i_bias,
           ml_f_bias, ml_norm_g, w_branch, w_out):
    nb, seq, d = x.shape
    ctx_len = ctx.shape[1]
    depth = w_mod.shape[0]
    assert d == 1024 and ctx_len % CHUNK == 0 and seq % CHUNK == 0 and seq % GRID_W == 0 and nb < 8

    sizes = (1536, 1024, 32, 1024, 1024, 1024, 1024, 1024, 1024, 1024, 1024, 8, 8, 3072)
    offs = np.concatenate([[0], np.cumsum(sizes)])
    seg = lambda n: w_in[:, :, offs[n]:offs[n + 1]]
    w_big = jnp.concatenate([seg(1), seg(3), seg(4), seg(5), seg(6), seg(7), seg(8), seg(9), seg(10),
                             seg(13), seg(0)], axis=-1).astype(BF16)
    w_small = jnp.concatenate([seg(2), seg(11), seg(12)], axis=-1)
    w_small = jnp.pad(w_small, ((0, 0), (0, 0), (0, LANES - w_small.shape[-1]))).astype(BF16)

    cc = jnp.concatenate([c, c_ctx[None, :], jnp.zeros((8 - nb - 1, d), F32)], axis=0)
    mod = _modulation(cc, w_mod, b_mod)

    tri, trit, e01, gm = _scan_constants()
    cos, sa, sb = _rope_tables(ctx_len, seq)
    wb = w_branch.astype(BF16)
    wo = w_out.astype(BF16)
    xall = jnp.concatenate([ctx, x], axis=1)

    for l in range(depth):
        lam_init = 0.8 - 0.6 * math.exp(-0.3 * l)
        p_big, p_small = _inproj(xall, mod[l], norm_g[l][None, :], w_big[l], w_small[l], ctx_len)
        yf, yb = _ssd(p_big, p_small, ssd_conv_w[l], ssd_conv_b[l][None, :],
                      _pad_row(ssd_dt_bias[l], 0), _pad_row(-jnp.exp(ssd_a_log[l]), 0),
                      jnp.repeat(ssd_d[l], SSD_HEAD_DIM)[None, :], tri, trit, e01, ctx_len)
        qn, kn = _qk_prep(p_big, cos, sa, sb, jnp.tile(diff_qn_g[l], 2)[None, :],
                          jnp.tile(diff_kn_g[l], 2)[None, :], gm)
        yd = _attention(qn, kn, p_big, diff_lambda[l], diff_qn_g[l][None, :], diff_kn_g[l][None, :],
                        diff_subln_g[l][None, :], ctx_len, lam_init)
        hf, hb = _mlstm(p_big, p_small, ml_conv_w[l], ml_conv_b[l][None, :],
                        _pad_row(ml_i_bias[l], SM_IG), _pad_row(ml_f_bias[l], SM_FG), tri, trit, ctx_len)
        xall = _merge(xall, mod[l], yf, yb, p_big, yd, hf, hb, ssd_norm_g[l][None, :], ml_norm_g[l][None, :],
                      wb[l], wo[l], ctx_len)
    return xall[:, ctx_len:]
```

```python
import functools
import math

import numpy as np
import jax
import jax.numpy as jnp
from jax import lax
from jax.experimental import pallas as pl
from jax.experimental.pallas import tpu as pltpu

F32 = jnp.float32
BF16 = jnp.bfloat16

GRID_W = 64
EPS = 1e-6
CONV_K = 3
SSD_HEADS = 16
SSD_HEAD_DIM = 64
SSD_GROUPS = 2
SSD_STATE = 128
DIFF_HEADS = 8
DIFF_HEAD_DIM = 64
ROPE_BASE = 10000.0
ML_HEADS = 4
ML_QK_DIM = 128
ML_V_DIM = 256
CHUNK = 128
LANES = 128
BF16_ROWS = 16

OFF_ZS, OFF_Q, OFF_K, OFF_V, OFF_ZD = 0, 1024, 2048, 3072, 4096
OFF_MQK, OFF_MV, OFF_MO, OFF_MZ, OFF_GATES, OFF_XBC = 5120, 6144, 7168, 8192, 9216, 12288
N_BIG = 13824
SM_IG, SM_FG = 32, 40

VMEM_LIMIT = 56 * 1024 * 1024


def _cparams(sem):
    return pltpu.CompilerParams(dimension_semantics=sem, vmem_limit_bytes=VMEM_LIMIT)


def _split3(v):
    h = v.astype(BF16)
    r = v - h.astype(F32)
    m = r.astype(BF16)
    l = (r - m.astype(F32)).astype(BF16)
    return h, m, l


def _dot(a, b):
    return jnp.dot(a, b, preferred_element_type=F32)


def _dot_nt(a, b):
    return lax.dot_general(a, b, (((1,), (1,)), ((), ())), preferred_element_type=F32)


def _dot_tn(a, b):
    return lax.dot_general(a, b, (((0,), (0,)), ((), ())), preferred_element_type=F32)


def _dot_exact_rhs01(v, m01):
    h, m, l = _split3(v)
    return _dot(h, m01) + _dot(m, m01) + _dot(l, m01)


def _dot_exact_lhs01(m01, v):
    h, m, l = _split3(v)
    return _dot(m01, h) + _dot(m01, m) + _dot(m01, l)


def _sigmoid(x):
    return 1.0 / (1.0 + jnp.exp(-x))


def _silu(x):
    return x * _sigmoid(x)


def _softplus(x):
    return jnp.maximum(x, 0.0) + jnp.log(1.0 + jnp.exp(-jnp.abs(x)))


def _mod_kernel(cc_ref, w_ref, b_ref, o_ref):
    a = _silu(cc_ref[...])
    w = w_ref[0]
    ah, am, _ = _split3(a)
    wh = w.astype(BF16)
    wm = (w - wh.astype(F32)).astype(BF16)
    o_ref[0] = _dot(ah, wh) + _dot(am, wh) + _dot(ah, wm) + b_ref[0]


def _modulation(cc, w_mod, b_mod):
    depth, d, d3 = w_mod.shape
    tn = 1024
    return pl.pallas_call(
        _mod_kernel,
        grid=(depth, d3 // tn),
        in_specs=[pl.BlockSpec((8, d), lambda l, j: (0, 0)),
                  pl.BlockSpec((1, d, tn), lambda l, j: (l, 0, j)),
                  pl.BlockSpec((1, 1, tn), lambda l, j: (l, 0, j))],
        out_specs=pl.BlockSpec((1, 8, tn), lambda l, j: (l, 0, j)),
        out_shape=jax.ShapeDtypeStruct((depth, 8, d3), F32),
        compiler_params=_cparams(("arbitrary", "arbitrary")),
        name="modulation",
    )(cc, w_mod, b_mod.reshape(depth, 1, d3))


def _inproj_kernel(x_ref, mod_ref, g_ref, w_ref, ws_ref, o_ref, os_ref, h_ref, *, lat_len, ctx_row):
    b, i, j = pl.program_id(0), pl.program_id(1), pl.program_id(2)
    tm, d = h_ref.shape

    @pl.when(j == 0)
    def _():
        x = x_ref[0]
        ms = jnp.mean(x * x, axis=-1, keepdims=True)
        y = x * lax.rsqrt(ms + EPS) * g_ref[...]
        row = i * tm + lax.broadcasted_iota(jnp.int32, (tm, 1), 0)
        is_ctx = row >= lat_len
        mlat = mod_ref[pl.ds(b, 1), :]
        mctx = mod_ref[pl.ds(ctx_row, 1), :]
        shift = jnp.where(is_ctx, mctx[:, :d], mlat[:, :d])
        scale = jnp.where(is_ctx, mctx[:, d:2 * d], mlat[:, d:2 * d])
        h = (y * (1.0 + scale) + shift).astype(BF16)
        h_ref[...] = h
        os_ref[0] = _dot(h, ws_ref[...])

    o_ref[0] = _dot(h_ref[...], w_ref[...]).astype(BF16)


def _row_tile(ta, target):
    best = BF16_ROWS
    for t in range(BF16_ROWS, target + 1, BF16_ROWS):
        if ta % t == 0:
            best = t
    return best


def _inproj(xall, mod_l, g, w_big, w_small, ctx_len):
    nb, ta, d = xall.shape
    tm = _row_tile(ta, 1056)
    tn = 1536
    kern = functools.partial(_inproj_kernel, lat_len=ta - ctx_len, ctx_row=nb)
    return pl.pallas_call(
        kern,
        grid=(nb, ta // tm, N_BIG // tn),
        in_specs=[pl.BlockSpec((1, tm, d), lambda b, i, j: (b, i, 0)),
                  pl.BlockSpec(mod_l.shape, lambda b, i, j: (0, 0)),
                  pl.BlockSpec((1, d), lambda b, i, j: (0, 0)),
                  pl.BlockSpec((d, tn), lambda b, i, j: (0, j)),
                  pl.BlockSpec((d, LANES), lambda b, i, j: (0, 0))],
        out_specs=[pl.BlockSpec((1, tm, tn), lambda b, i, j: (b, i, j)),
                   pl.BlockSpec((1, tm, LANES), lambda b, i, j: (b, i, 0))],
        out_shape=[jax.ShapeDtypeStruct((nb, ta, N_BIG), BF16),
                   jax.ShapeDtypeStruct((nb, ta, LANES), F32)],
        scratch_shapes=[pltpu.VMEM((tm, d), BF16)],
        compiler_params=_cparams(("arbitrary", "arbitrary", "arbitrary")),
        name="inproj",
    )(xall, mod_l, g, w_big, w_small)


def _fwd_chunk(i, ncc, nch):
    return jnp.where(i < ncc, nch - ncc + i, i - ncc)


def _bwd_chunk(i, ncc, nch):
    del ncc
    return nch - 1 - i


def _conv_silu(x_ref, xp_ref, xn_ref, cw_ref, cb_ref, c, ncc, nch):
    x = x_ref[0].astype(F32)
    L = x.shape[0]
    nlat = nch - ncc
    has_prev = jnp.logical_and(c != 0, c != nlat)
    has_next = jnp.logical_and(c != nlat - 1, c != nch - 1)
    prow = jnp.where(has_prev, xp_ref[0][BF16_ROWS - 1:BF16_ROWS, :].astype(F32), 0.0)
    nrow = jnp.where(has_next, xn_ref[0][0:1, :].astype(F32), 0.0)
    rid = lax.broadcasted_iota(jnp.int32, x.shape, 0)
    xprev = jnp.where(rid == 0, prow, pltpu.roll(x, 1, 0))
    xnext = jnp.where(rid == L - 1, nrow, pltpu.roll(x, L - 1, 0))
    u = xprev * cw_ref[0:1, :] + x * cw_ref[1:2, :] + xnext * cw_ref[2:3, :] + cb_ref[...]
    return _silu(u)


def _chunk_specs(width, col_block, ncc, nch, nrow16):
    per = CHUNK // BF16_ROWS

    def mk(cfun):
        main = pl.BlockSpec((1, CHUNK, width), lambda b, i: (b, cfun(i), col_block))
        prev = pl.BlockSpec((1, BF16_ROWS, width),
                            lambda b, i: (b, jnp.maximum(cfun(i) * per - 1, 0), col_block))
        nxt = pl.BlockSpec((1, BF16_ROWS, width),
                           lambda b, i: (b, jnp.minimum((cfun(i) + 1) * per, nrow16 - 1), col_block))
        return [main, prev, nxt]

    return mk(lambda i: _fwd_chunk(i, ncc, nch)) + mk(lambda i: _bwd_chunk(i, ncc, nch))


def _full_spec(a):
    nd = a.ndim
    return pl.BlockSpec(a.shape, lambda b, i: (0,) * nd)


def _ssd_direction(d, c, x_ref, xp_ref, xn_ref, ps_ref, cw_ref, cb_ref, dtb_ref, a_ref, dsk_ref,
                   tri_ref, trit_ref, e_ref, y_ref, s_ref, ncc, nch):
    L = CHUNK
    hp = SSD_HEADS * SSD_HEAD_DIM
    gn = SSD_GROUPS * SSD_STATE
    gw = hp // SSD_GROUPS
    u = _conv_silu(x_ref, xp_ref, xn_ref, cw_ref, cb_ref, c, ncc, nch)
    xs = u[:, :hp]
    bm = u[:, hp:hp + gn].astype(BF16)
    cm = u[:, hp + gn:].astype(BF16)

    dt = _softplus(ps_ref[0] + dtb_ref[...])
    a = dt * a_ref[...]
    tri = tri_ref[d]
    acs = _dot_exact_lhs01(tri, a)
    acs_t = _dot_exact_rhs01(a.T, trit_ref[d])
    last = L - 1 if d == 0 else 0
    acs_last = acs[last:last + 1, :]
    dend = jnp.exp(acs_last - acs)
    eacs = jnp.exp(acs)

    e01 = e_ref[d]
    dt_e = _dot_exact_rhs01(dt, e01)
    dtend_e = _dot_exact_rhs01(dt * dend, e01)
    eacs_e = _dot_exact_rhs01(eacs, e01)
    xd = xs * dt_e
    xde = (xs * dtend_e).astype(BF16)
    cd_e = eacs_e[last:last + 1, :]

    mask = tri > 0
    lane = lax.broadcasted_iota(jnp.int32, (L, LANES), 1)
    left = lane < SSD_HEAD_DIM
    heads_per_group = SSD_HEADS // SSD_GROUPS
    ys = []
    for g in range(SSD_GROUPS):
        cg = cm[:, g * SSD_STATE:(g + 1) * SSD_STATE]
        bg = bm[:, g * SSD_STATE:(g + 1) * SSD_STATE]
        sg = _dot_nt(cg, bg)
        s_prev = s_ref[d, g]
        y_off = _dot(cg, s_prev.astype(BF16)) * eacs_e[:, g * gw:(g + 1) * gw]
        pairs = []
        for k in range(heads_per_group // 2):
            h0 = g * heads_per_group + 2 * k
            ms = []
            for h in (h0, h0 + 1):
                j = SSD_HEADS * d + h
                seg = acs[:, j:j + 1] - acs_t[j:j + 1, :]
                ms.append((sg * jnp.exp(jnp.where(mask, seg, -jnp.inf))).astype(BF16))
            xp = xd[:, h0 * SSD_HEAD_DIM:(h0 + 2) * SSD_HEAD_DIM]
            rhs = jnp.concatenate([jnp.where(left, xp, 0.0), jnp.where(left, 0.0, xp)], axis=0).astype(BF16)
            pairs.append(_dot(jnp.concatenate(ms, axis=1), rhs))
        ys.append(jnp.concatenate(pairs, axis=1) + y_off)
        s_ref[d, g] = s_prev * cd_e[:, g * gw:(g + 1) * gw] + _dot_tn(bg, xde[:, g * gw:(g + 1) * gw])
    y = jnp.concatenate(ys, axis=1)
    if d == 0:
        y = y + dsk_ref[...] * xs
    y_ref[0, 0] = y.astype(BF16)


def _ssd_kernel(xf, xfp, xfn, xb, xbp, xbn, psf, psb, cw, cb, dtb, aneg, dsk, tri, trit, e01,
                yf, yb, s_ref, *, ncc, nch):
    i = pl.program_id(1)

    @pl.when(i == 0)
    def _():
        s_ref[...] = jnp.zeros_like(s_ref)

    _ssd_direction(0, _fwd_chunk(i, ncc, nch), xf, xfp, xfn, psf, cw, cb, dtb, aneg, dsk, tri, trit, e01,
                   yf, s_ref, ncc, nch)
    _ssd_direction(1, _bwd_chunk(i, ncc, nch), xb, xbp, xbn, psb, cw, cb, dtb, aneg, dsk, tri, trit, e01,
                   yb, s_ref, ncc, nch)


def _ssd(p_big, p_small, cw, cb, dtb, aneg, dsk, tri, trit, e01, ctx_len):
    nb, ta, _ = p_big.shape
    ncc, nch = ctx_len // CHUNK, ta // CHUNK
    hp = SSD_HEADS * SSD_HEAD_DIM
    width = hp + 2 * SSD_GROUPS * SSD_STATE
    xspecs = _chunk_specs(width, OFF_XBC // width, ncc, nch, ta // BF16_ROWS)
    ps_f = pl.BlockSpec((1, CHUNK, LANES), lambda b, i: (b, _fwd_chunk(i, ncc, nch), 0))
    ps_b = pl.BlockSpec((1, CHUNK, LANES), lambda b, i: (b, _bwd_chunk(i, ncc, nch), 0))
    consts = [cw, cb, dtb, aneg, dsk, tri, trit, e01]
    kern = functools.partial(_ssd_kernel, ncc=ncc, nch=nch)
    yf, yb = pl.pallas_call(
        kern,
        grid=(nb, nch),
        in_specs=xspecs + [ps_f, ps_b] + [_full_spec(a) for a in consts],
        out_specs=[pl.BlockSpec((1, 1, CHUNK, hp), lambda b, i: (0, b, _fwd_chunk(i, ncc, nch), 0)),
                   pl.BlockSpec((1, 1, CHUNK, hp), lambda b, i: (0, b, _bwd_chunk(i, ncc, nch), 0))],
        out_shape=[jax.ShapeDtypeStruct((1, nb, ta, hp), BF16)] * 2,
        scratch_shapes=[pltpu.VMEM((2, SSD_GROUPS, SSD_STATE, hp // SSD_GROUPS), F32)],
        compiler_params=_cparams(("arbitrary", "arbitrary")),
        name="ssd_scan",
    )(p_big, p_big, p_big, p_big, p_big, p_big, p_small, p_small, *consts)
    return yf[0], yb[0]


def _ml_direction(d, c, qk_ref, qkp_ref, qkn_ref, v_ref, ps_ref, cw_ref, cb_ref, ib_ref, fb_ref,
                  tri_ref, trit_ref, h_ref, c_ref, n_ref, m_ref, ncc, nch):
    L = CHUNK
    hq = ML_HEADS * ML_QK_DIM
    u = _conv_silu(qk_ref, qkp_ref, qkn_ref, cw_ref, cb_ref, c, ncc, nch)
    ps = ps_ref[0]
    li = ps + ib_ref[...]
    lf = -_softplus(-(ps + fb_ref[...]))
    tri = tri_ref[d]
    trit = trit_ref[d]
    bcum = _dot_exact_lhs01(tri, lf)
    bcum_t = _dot_exact_rhs01(lf.T, trit)
    li_t = li.T
    mask = tri > 0
    last = L - 1 if d == 0 else 0
    outs = []
    for h in range(ML_HEADS):
        ji = SM_IG + ML_HEADS * d + h
        jf = SM_FG + ML_HEADS * d + h
        q = u[:, h * ML_QK_DIM:(h + 1) * ML_QK_DIM]
        k = u[:, hq + h * ML_QK_DIM:hq + (h + 1) * ML_QK_DIM] * (ML_QK_DIM ** -0.5)
        v = v_ref[0][:, h * ML_V_DIM:(h + 1) * ML_V_DIM]
        qb = q.astype(BF16)
        kb = k.astype(BF16)
        bc_col = bcum[:, jf:jf + 1]
        bc_row = bcum_t[jf:jf + 1, :]
        li_col = li[:, ji:ji + 1]
        li_row = li_t[ji:ji + 1, :]
        b_last = bc_col[last:last + 1, :]
        c_prev = c_ref[d, h]
        n_prev = n_ref[d, h]
        m_prev = m_ref[d, h][:, 0:1]

        dm = jnp.where(mask, bc_col - bc_row + li_row, -jnp.inf)
        e = bc_col + m_prev
        m_t = jnp.maximum(e, jnp.max(dm, axis=-1, keepdims=True))
        wts = jnp.exp(dm - m_t) * _dot_nt(qb, kb)
        sc = jnp.exp(e - m_t)
        num = _dot(wts.astype(BF16), v) + _dot(qb, c_prev.astype(BF16)) * sc
        den = jnp.sum(wts, axis=-1, keepdims=True) + jnp.sum(q * n_prev, axis=-1, keepdims=True) * sc
        den = jnp.maximum(jnp.abs(den), jnp.exp(-m_t))
        outs.append(num / den)

        gl = b_last - bc_col + li_col
        m_loc = jnp.max(gl, axis=0, keepdims=True)
        w = jnp.exp(gl - m_loc)
        kw = k * w
        c_loc = _dot_tn(kw.astype(BF16), v)
        n_loc = jnp.sum(kw, axis=0, keepdims=True)
        m_new = jnp.maximum(b_last + m_prev, m_loc)
        a_s = jnp.exp(b_last + m_prev - m_new)
        s_s = jnp.exp(m_loc - m_new)
        c_ref[d, h] = a_s * c_prev + s_s * c_loc
        n_ref[d, h] = a_s * n_prev + s_s * n_loc
        m_ref[d, h] = jnp.broadcast_to(m_new, (1, LANES))
    h_ref[0, 0] = jnp.concatenate(outs, axis=1).astype(BF16)


def _ml_kernel(qf, qfp, qfn, qb, qbp, qbn, vf, vb, psf, psb, cw, cb, ib, fb, tri, trit,
               hf, hb, c_ref, n_ref, m_ref, *, ncc, nch):
    i = pl.program_id(1)

    @pl.when(i == 0)
    def _():
        c_ref[...] = jnp.zeros_like(c_ref)
        n_ref[...] = jnp.zeros_like(n_ref)
        m_ref[...] = jnp.zeros_like(m_ref)

    _ml_direction(0, _fwd_chunk(i, ncc, nch), qf, qfp, qfn, vf, psf, cw, cb, ib, fb, tri, trit, hf,
                  c_ref, n_ref, m_ref, ncc, nch)
    _ml_direction(1, _bwd_chunk(i, ncc, nch), qb, qbp, qbn, vb, psb, cw, cb, ib, fb, tri, trit, hb,
                  c_ref, n_ref, m_ref, ncc, nch)


def _mlstm(p_big, p_small, cw, cb, ib, fb, tri, trit, ctx_len):
    nb, ta, _ = p_big.shape
    ncc, nch = ctx_len // CHUNK, ta // CHUNK
    wq = 2 * ML_HEADS * ML_QK_DIM
    wv = ML_HEADS * ML_V_DIM
    qspecs = _chunk_specs(wq, OFF_MQK // wq, ncc, nch, ta // BF16_ROWS)
    v_f = pl.BlockSpec((1, CHUNK, wv), lambda b, i: (b, _fwd_chunk(i, ncc, nch), OFF_MV // wv))
    v_b = pl.BlockSpec((1, CHUNK, wv), lambda b, i: (b, _bwd_chunk(i, ncc, nch), OFF_MV // wv))
    ps_f = pl.BlockSpec((1, CHUNK, LANES), lambda b, i: (b, _fwd_chunk(i, ncc, nch), 0))
    ps_b = pl.BlockSpec((1, CHUNK, LANES), lambda b, i: (b, _bwd_chunk(i, ncc, nch), 0))
    consts = [cw, cb, ib, fb, tri, trit]
    kern = functools.partial(_ml_kernel, ncc=ncc, nch=nch)
    hf, hb = pl.pallas_call(
        kern,
        grid=(nb, nch),
        in_specs=qspecs + [v_f, v_b, ps_f, ps_b] + [_full_spec(a) for a in consts],
        out_specs=[pl.BlockSpec((1, 1, CHUNK, wv), lambda b, i: (0, b, _fwd_chunk(i, ncc, nch), 0)),
                   pl.BlockSpec((1, 1, CHUNK, wv), lambda b, i: (0, b, _bwd_chunk(i, ncc, nch), 0))],
        out_shape=[jax.ShapeDtypeStruct((1, nb, ta, wv), BF16)] * 2,
        scratch_shapes=[pltpu.VMEM((2, ML_HEADS, ML_QK_DIM, ML_V_DIM), F32),
                        pltpu.VMEM((2, ML_HEADS, 1, ML_QK_DIM), F32),
                        pltpu.VMEM((2, ML_HEADS, 1, LANES), F32)],
        compiler_params=_cparams(("arbitrary", "arbitrary")),
        name="mlstm_scan",
    )(p_big, p_big, p_big, p_big, p_big, p_big, p_big, p_big, p_small, p_small, *consts)
    return hf[0], hb[0]


def _qk_prep_kernel(q_ref, k_ref, cos_ref, sa_ref, sb_ref, gq_ref, gk_ref, gm_ref, qo_ref, ko_ref, *, qscale):
    cos, sa, sb = cos_ref[...], sa_ref[...], sb_ref[...]
    gm = gm_ref[...]
    nheads = q_ref.shape[2] // LANES

    def prep(t, g, mult):
        t = t.astype(F32)
        sq = t * t
        sh = sq.astype(BF16)
        sm = (sq - sh.astype(F32)).astype(BF16)
        ms = (_dot(sh, gm) + _dot(sm, gm)) * (1.0 / DIFF_HEAD_DIM)
        y = t * lax.rsqrt(ms + EPS) * g
        r = y * cos + pltpu.roll(y, LANES - 16, 1) * sa + pltpu.roll(y, 16, 1) * sb
        return (r * mult).astype(BF16)

    for h in range(nheads):
        sl = slice(h * LANES, (h + 1) * LANES)
        qo_ref[0, :, sl] = prep(q_ref[0, :, sl], gq_ref[...], qscale)
        ko_ref[0, :, sl] = prep(k_ref[0, :, sl], gk_ref[...], 1.0)


def _qk_prep(p_big, cos, sa, sb, gq, gk, gm):
    nb, ta, _ = p_big.shape
    w = DIFF_HEADS * 2 * DIFF_HEAD_DIM
    tm = _row_tile(ta, 1056)
    qscale = DIFF_HEAD_DIM ** -0.5 * math.log2(math.e)
    row = lambda b, i: (i, 0)
    return pl.pallas_call(
        functools.partial(_qk_prep_kernel, qscale=qscale),
        grid=(nb, ta // tm),
        in_specs=[pl.BlockSpec((1, tm, w), lambda b, i: (b, i, OFF_Q // w)),
                  pl.BlockSpec((1, tm, w), lambda b, i: (b, i, OFF_K // w)),
                  pl.BlockSpec((tm, LANES), row), pl.BlockSpec((tm, LANES), row), pl.BlockSpec((tm, LANES), row),
                  _full_spec(gq), _full_spec(gk), _full_spec(gm)],
        out_specs=[pl.BlockSpec((1, tm, w), lambda b, i: (b, i, 0))] * 2,
        out_shape=[jax.ShapeDtypeStruct((nb, ta, w), BF16)] * 2,
        compiler_params=_cparams(("arbitrary", "arbitrary")),
        name="qk_prep",
    )(p_big, p_big, cos, sa, sb, gq, gk, gm)


def _attn_kernel(q_ref, k_ref, v_ref, z_ref, lam_ref, gq_ref, gk_ref, sg_ref, *rest, tk, lam_init):
    o_ref, vp_ref, e_ref = rest[-3:]
    qi = pl.program_id(2)
    nkeys = k_ref.shape[1]
    dh = DIFF_HEAD_DIM

    @pl.when(qi == 0)
    def _():
        vp_ref[:, :LANES] = v_ref[0]
        vp_ref[:, LANES:] = jnp.ones((nkeys, LANES), BF16)

    lp = lam_ref[...]
    lam = (jnp.exp(jnp.sum(lp[0:1] * lp[1:2], axis=-1, keepdims=True))
           - jnp.exp(jnp.sum(lp[2:3] * lp[3:4], axis=-1, keepdims=True)) + lam_init)
    shift = (jnp.max(jnp.abs(gq_ref[...]), axis=-1, keepdims=True)
             * jnp.max(jnp.abs(gk_ref[...]), axis=-1, keepdims=True)
             * (dh * dh ** -0.5 * math.log2(math.e)))

    q = q_ref[0]
    lane = lax.broadcasted_iota(jnp.int32, q.shape, 1)
    zero = jnp.zeros_like(q)
    acc = []
    for c, qc in enumerate((jnp.where(lane < dh, q, zero), jnp.where(lane < dh, zero, q))):
        for j in range(nkeys // tk):
            s = _dot_nt(qc, k_ref[0, j * tk:(j + 1) * tk, :])
            e_ref[c, :, j * tk:(j + 1) * tk] = jnp.exp2(s - shift).astype(BF16)
        acc.append(_dot(e_ref[c], vp_ref[...]))
    a0, a1 = acc
    o = a0[:, :LANES] / a0[:, LANES:] - lam * (a1[:, :LANES] / a1[:, LANES:])
    ms = jnp.mean(o * o, axis=-1, keepdims=True)
    o = o * lax.rsqrt(ms + EPS) * sg_ref[...] * (1.0 - lam_init)
    o_ref[0] = (o * _silu(z_ref[0].astype(F32))).astype(BF16)


def _attention(qn, kn, p_big, lam_p, gq, gk, sg, ctx_len, lam_init):
    nb, ta, w = qn.shape
    seq = ta - ctx_len
    tk = 256
    tq = 512 if seq % 512 == 0 else 256
    assert seq % tq == 0 and seq % ctx_len == 0 and ta % tk == 0 and ctx_len % tk == 0
    kern = functools.partial(_attn_kernel, tk=tk, lam_init=lam_init)
    full = lambda a: pl.BlockSpec(a.shape, lambda b, h, i: (0,) * a.ndim)
    consts = [lam_p, gq, gk, sg]

    def call(tq_, nkeys, row_block, key_block, nq, prev):
        in_specs = [pl.BlockSpec((1, tq_, LANES), lambda b, h, i: (b, row_block + i, h)),
                    pl.BlockSpec((1, nkeys, LANES), lambda b, h, i: (b, key_block, h)),
                    pl.BlockSpec((1, nkeys, LANES), lambda b, h, i: (b, key_block, OFF_V // LANES + h)),
                    pl.BlockSpec((1, tq_, LANES), lambda b, h, i: (b, row_block + i, OFF_ZD // LANES + h))]
        in_specs += [full(a) for a in consts]
        args = [qn, kn, p_big, p_big] + consts
        aliases = {}
        if prev is not None:
            in_specs.append(pl.BlockSpec(memory_space=pl.ANY))
            aliases = {len(args): 0}
            args.append(prev)
        return pl.pallas_call(
            kern,
            grid=(nb, DIFF_HEADS, nq),
            in_specs=in_specs,
            out_specs=pl.BlockSpec((1, tq_, LANES), lambda b, h, i: (b, row_block + i, h)),
            out_shape=jax.ShapeDtypeStruct((nb, ta, w), BF16),
            scratch_shapes=[pltpu.VMEM((nkeys, 2 * LANES), BF16), pltpu.VMEM((2, tq_, nkeys), BF16)],
            input_output_aliases=aliases,
            compiler_params=_cparams(("arbitrary", "arbitrary", "arbitrary")),
            name="diff_attention" if prev is None else "diff_attention_ctx",
        )(*args)

    yd = call(tq, ta, 0, 0, seq // tq, None)
    return call(ctx_len, ctx_len, seq // ctx_len, seq // ctx_len, 1, yd)


def _merge_kernel(x_ref, mod_ref, yf_ref, yb_ref, zs_ref, yd_ref, hf_ref, hb_ref, mo_ref, mz_ref, gt_ref,
                  sng_ref, mng_ref, wb_ref, wo_ref, o_ref, *, lat_len, ctx_row):
    b, i = pl.program_id(0), pl.program_id(1)
    tm, d = x_ref.shape[1], x_ref.shape[2]

    ya = (yf_ref[0].astype(F32) + yb_ref[0].astype(F32)) * _silu(zs_ref[0].astype(F32))
    ya = ya * lax.rsqrt(jnp.mean(ya * ya, axis=-1, keepdims=True) + EPS) * sng_ref[...]

    hm = (hf_ref[0].astype(F32) + hb_ref[0].astype(F32)) * _sigmoid(mo_ref[0].astype(F32))
    parts = []
    for h in range(ML_HEADS):
        t = hm[:, h * ML_V_DIM:(h + 1) * ML_V_DIM]
        parts.append(t * lax.rsqrt(jnp.mean(t * t, axis=-1, keepdims=True) + EPS))
    yc = jnp.concatenate(parts, axis=1) * mng_ref[...] * _silu(mz_ref[0].astype(F32))

    g = _sigmoid(gt_ref[0].astype(F32))
    mixed = (g[:, :d] * _dot(ya.astype(BF16), wb_ref[0])
             + g[:, d:2 * d] * _dot(yd_ref[0], wb_ref[1])
             + g[:, 2 * d:] * _dot(yc.astype(BF16), wb_ref[2]))
    out = _dot(mixed.astype(BF16), wo_ref[...])

    row = i * tm + lax.broadcasted_iota(jnp.int32, (tm, 1), 0)
    is_ctx = row >= lat_len
    gate = jnp.where(is_ctx, mod_ref[pl.ds(ctx_row, 1), 2 * d:], mod_ref[pl.ds(b, 1), 2 * d:])
    o_ref[0] = x_ref[0] + gate * out


def _merge(xall, mod_l, yf, yb, p_big, yd, hf, hb, sng, mng, wb, wo, ctx_len):
    nb, ta, d = xall.shape
    tm = _row_tile(ta, 352)
    kern = functools.partial(_merge_kernel, lat_len=ta - ctx_len, ctx_row=nb)
    blk = lambda col: pl.BlockSpec((1, tm, d), lambda b, i: (b, i, col))
    full = lambda a: pl.BlockSpec(a.shape, lambda b, i: (0,) * a.ndim)
    return pl.pallas_call(
        kern,
        grid=(nb, ta // tm),
        in_specs=[blk(0), full(mod_l), blk(0), blk(0), blk(OFF_ZS // d), blk(0), blk(0), blk(0),
                  blk(OFF_MO // d), blk(OFF_MZ // d),
                  pl.BlockSpec((1, tm, 3 * d), lambda b, i: (b, i, OFF_GATES // (3 * d))),
                  full(sng), full(mng), full(wb), full(wo)],
        out_specs=blk(0),
        out_shape=jax.ShapeDtypeStruct((nb, ta, d), F32),
        compiler_params=_cparams(("arbitrary", "arbitrary")),
        name="merge",
    )(xall, mod_l, yf, yb, p_big, yd, hf, hb, p_big, p_big, p_big, sng, mng, wb, wo)


def _rope_tables(ctx_len, seq):
    rows = seq // GRID_W
    row = jnp.repeat(jnp.arange(rows, dtype=F32), GRID_W)
    col = jnp.tile(jnp.arange(GRID_W, dtype=F32), rows)
    half = DIFF_HEAD_DIM // 2
    inv_freq = ROPE_BASE ** (-jnp.arange(0, half, 2, dtype=F32) / half)
    ang_r = row[:, None] * inv_freq
    ang_c = col[:, None] * inv_freq
    cos = jnp.concatenate([jnp.cos(ang_r), jnp.cos(ang_r), jnp.cos(ang_c), jnp.cos(ang_c)], axis=-1)
    sin = jnp.concatenate([jnp.sin(ang_r), jnp.sin(ang_r), jnp.sin(ang_c), jnp.sin(ang_c)], axis=-1)
    cos = jnp.concatenate([cos, jnp.ones((ctx_len, DIFF_HEAD_DIM), F32)], axis=0)
    sin = jnp.concatenate([sin, jnp.zeros((ctx_len, DIFF_HEAD_DIM), F32)], axis=0)
    cos = jnp.tile(cos, (1, LANES // DIFF_HEAD_DIM))
    sin = jnp.tile(sin, (1, LANES // DIFF_HEAD_DIM))
    first = (jnp.arange(LANES) % (DIFF_HEAD_DIM // 2)) < DIFF_HEAD_DIM // 4
    return cos, jnp.where(first, -sin, 0.0), jnp.where(first, 0.0, sin)


def _scan_constants():
    idx = np.arange(CHUNK)
    tri_f = (idx[None, :] <= idx[:, None]).astype(np.float32)
    tri = np.stack([tri_f, tri_f.T])
    trit = np.stack([tri_f.T, tri_f])
    e01 = np.zeros((2, LANES, SSD_HEADS * SSD_HEAD_DIM), np.float32)
    for d in range(2):
        for h in range(SSD_HEADS):
            e01[d, SSD_HEADS * d + h, h * SSD_HEAD_DIM:(h + 1) * SSD_HEAD_DIM] = 1.0
    half = np.arange(LANES) // DIFF_HEAD_DIM
    gm = (half[:, None] == half[None, :]).astype(np.float32)
    return (jnp.asarray(tri, BF16), jnp.asarray(trit, BF16), jnp.asarray(e01, BF16), jnp.asarray(gm, BF16))


def _pad_row(v, offset):
    v = v.reshape(1, -1).astype(F32)
    return jnp.pad(v, ((0, 0), (offset, LANES - offset - v.shape[1])))


def kernel(x, c, ctx, c_ctx, w_mod, b_mod, norm_g, w_in, ssd_conv_w, ssd_conv_b, ssd_a_log, ssd_dt_bias, ssd_d,
           ssd_norm_g, diff_qn_g, diff_kn_g, diff_lambda, diff_subln_g, ml_conv_w, ml_conv_b, ml_i_bias,
           ml_f_bias, ml_norm_g, w_branch, w_out):
    nb, seq, d = x.shape
    ctx_len = ctx.shape[1]
    depth = w_mod.shape[0]
    assert d == 1024 and ctx_len % CHUNK == 0 and seq % CHUNK == 0 and seq % GRID_W == 0 and nb < 8

    sizes = (1536, 1024, 32, 1024, 1024, 1024, 1024, 1024, 1024, 1024, 1024, 8, 8, 3072)
    offs = np.concatenate([[0], np.cumsum(sizes)])
    seg = lambda n: w_in[:, :, offs[n]:offs[n + 1]]
    w_big = jnp.concatenate([seg(1), seg(3), seg(4), seg(5), seg(6), seg(7), seg(8), seg(9), seg(10),
                             seg(13), seg(0)], axis=-1).astype(BF16)
    w_small = jnp.concatenate([seg(2), seg(11), seg(12)], axis=-1)
    w_small = jnp.pad(w_small, ((0, 0), (0, 0), (0, LANES - w_small.shape[-1]))).astype(BF16)

    cc = jnp.concatenate([c, c_ctx[None, :], jnp.zeros((8 - nb - 1, d), F32)], axis=0)
    mod = _modulation(cc, w_mod, b_mod)

    tri, trit, e01, gm = _scan_constants()
    cos, sa, sb = _rope_tables(ctx_len, seq)
    wb = w_branch.astype(BF16)
    wo = w_out.astype(BF16)
    xall = jnp.concatenate([x, ctx], axis=1)

    for l in range(depth):
        lam_init = 0.8 - 0.6 * math.exp(-0.3 * l)
        p_big, p_small = _inproj(xall, mod[l], norm_g[l][None, :], w_big[l], w_small[l], ctx_len)
        yf, yb = _ssd(p_big, p_small, ssd_conv_w[l], ssd_conv_b[l][None, :],
                      _pad_row(ssd_dt_bias[l], 0), _pad_row(-jnp.exp(ssd_a_log[l]), 0),
                      jnp.repeat(ssd_d[l], SSD_HEAD_DIM)[None, :], tri, trit, e01, ctx_len)
        qn, kn = _qk_prep(p_big, cos, sa, sb, jnp.tile(diff_qn_g[l], 2)[None, :],
                          jnp.tile(diff_kn_g[l], 2)[None, :], gm)
        yd = _attention(qn, kn, p_big, diff_lambda[l], diff_qn_g[l][None, :], diff_kn_g[l][None, :],
                        diff_subln_g[l][None, :], ctx_len, lam_init)
        hf, hb = _mlstm(p_big, p_small, ml_conv_w[l], ml_conv_b[l][None, :],
                        _pad_row(ml_i_bias[l], SM_IG), _pad_row(ml_f_bias[l], SM_FG), tri, trit, ctx_len)
        xall = _merge(xall, mod[l], yf, yb, p_big, yd, hf, hb, ssd_norm_g[l][None, :], ml_norm_g[l][None, :],
                      wb[l], wo[l], ctx_len)
    return xall[:, :seq]
```

```python
import functools
import math

import numpy as np
import jax
import jax.numpy as jnp
from jax import lax
from jax.experimental import pallas as pl
from jax.experimental.pallas import tpu as pltpu

F32 = jnp.float32
BF16 = jnp.bfloat16

GRID_W = 64
EPS = 1e-6
CONV_K = 3
SSD_HEADS = 16
SSD_HEAD_DIM = 64
SSD_GROUPS = 2
SSD_STATE = 128
DIFF_HEADS = 8
DIFF_HEAD_DIM = 64
ROPE_BASE = 10000.0
ML_HEADS = 4
ML_QK_DIM = 128
ML_V_DIM = 256
CHUNK = 128
LANES = 128
BF16_ROWS = 16

OFF_ZS, OFF_Q, OFF_K, OFF_V, OFF_ZD = 0, 1024, 2048, 3072, 4096
OFF_MQK, OFF_MV, OFF_MO, OFF_MZ, OFF_GATES, OFF_XBC = 5120, 6144, 7168, 8192, 9216, 12288
N_BIG = 13824
SM_IG, SM_FG = 32, 40

VMEM_LIMIT = 56 * 1024 * 1024


def _cparams(sem):
    return pltpu.CompilerParams(dimension_semantics=sem, vmem_limit_bytes=VMEM_LIMIT)


def _split3(v):
    h = v.astype(BF16)
    r = v - h.astype(F32)
    m = r.astype(BF16)
    l = (r - m.astype(F32)).astype(BF16)
    return h, m, l


def _dot(a, b):
    return jnp.dot(a, b, preferred_element_type=F32)


def _dot_nt(a, b):
    return lax.dot_general(a, b, (((1,), (1,)), ((), ())), preferred_element_type=F32)


def _dot_tn(a, b):
    return lax.dot_general(a, b, (((0,), (0,)), ((), ())), preferred_element_type=F32)


def _dot_exact_rhs01(v, m01):
    h, m, l = _split3(v)
    return _dot(h, m01) + _dot(m, m01) + _dot(l, m01)


def _expand_rhs01(v, m01):
    h = v.astype(BF16)
    m = (v - h.astype(F32)).astype(BF16)
    return _dot(h, m01) + _dot(m, m01)


def _dot_exact_lhs01(m01, v):
    h, m, l = _split3(v)
    return _dot(m01, h) + _dot(m01, m) + _dot(m01, l)


def _sigmoid(x):
    return 1.0 / (1.0 + jnp.exp(-x))


def _silu(x):
    return x * _sigmoid(x)


def _softplus(x):
    return jnp.maximum(x, 0.0) + jnp.log(1.0 + jnp.exp(-jnp.abs(x)))


def _mod_kernel(cc_ref, w_ref, b_ref, o_ref):
    a = _silu(cc_ref[...])
    w = w_ref[0]
    ah, am, _ = _split3(a)
    wh = w.astype(BF16)
    wm = (w - wh.astype(F32)).astype(BF16)
    o_ref[0] = _dot(ah, wh) + _dot(am, wh) + _dot(ah, wm) + b_ref[0]


def _modulation(cc, w_mod, b_mod):
    depth, d, d3 = w_mod.shape
    tn = 1024
    return pl.pallas_call(
        _mod_kernel,
        grid=(depth, d3 // tn),
        in_specs=[pl.BlockSpec((8, d), lambda l, j: (0, 0)),
                  pl.BlockSpec((1, d, tn), lambda l, j: (l, 0, j)),
                  pl.BlockSpec((1, 1, tn), lambda l, j: (l, 0, j))],
        out_specs=pl.BlockSpec((1, 8, tn), lambda l, j: (l, 0, j)),
        out_shape=jax.ShapeDtypeStruct((depth, 8, d3), F32),
        compiler_params=_cparams(("arbitrary", "arbitrary")),
        name="modulation",
    )(cc, w_mod, b_mod.reshape(depth, 1, d3))


def _inproj_kernel(x_ref, mod_ref, g_ref, w_ref, ws_ref, o_ref, os_ref, h_ref, *, lat_len, ctx_row):
    b, i, j = pl.program_id(0), pl.program_id(1), pl.program_id(2)
    tm, d = h_ref.shape

    @pl.when(j == 0)
    def _():
        x = x_ref[0]
        ms = jnp.mean(x * x, axis=-1, keepdims=True)
        y = x * lax.rsqrt(ms + EPS) * g_ref[...]
        row = i * tm + lax.broadcasted_iota(jnp.int32, (tm, 1), 0)
        is_ctx = row >= lat_len
        mlat = mod_ref[pl.ds(b, 1), :]
        mctx = mod_ref[pl.ds(ctx_row, 1), :]
        shift = jnp.where(is_ctx, mctx[:, :d], mlat[:, :d])
        scale = jnp.where(is_ctx, mctx[:, d:2 * d], mlat[:, d:2 * d])
        h = (y * (1.0 + scale) + shift).astype(BF16)
        h_ref[...] = h
        os_ref[0] = _dot(h, ws_ref[...])

    o_ref[0] = _dot(h_ref[...], w_ref[...]).astype(BF16)


def _row_tile(ta, target):
    best = BF16_ROWS
    for t in range(BF16_ROWS, target + 1, BF16_ROWS):
        if ta % t == 0:
            best = t
    return best


def _inproj(xall, mod_l, g, w_big, w_small, ctx_len):
    nb, ta, d = xall.shape
    tm = _row_tile(ta, 1056)
    tn = 1536
    kern = functools.partial(_inproj_kernel, lat_len=ta - ctx_len, ctx_row=nb)
    return pl.pallas_call(
        kern,
        grid=(nb, ta // tm, N_BIG // tn),
        in_specs=[pl.BlockSpec((1, tm, d), lambda b, i, j: (b, i, 0)),
                  pl.BlockSpec(mod_l.shape, lambda b, i, j: (0, 0)),
                  pl.BlockSpec((1, d), lambda b, i, j: (0, 0)),
                  pl.BlockSpec((d, tn), lambda b, i, j: (0, j)),
                  pl.BlockSpec((d, LANES), lambda b, i, j: (0, 0))],
        out_specs=[pl.BlockSpec((1, tm, tn), lambda b, i, j: (b, i, j)),
                   pl.BlockSpec((1, tm, LANES), lambda b, i, j: (b, i, 0))],
        out_shape=[jax.ShapeDtypeStruct((nb, ta, N_BIG), BF16),
                   jax.ShapeDtypeStruct((nb, ta, LANES), F32)],
        scratch_shapes=[pltpu.VMEM((tm, d), BF16)],
        compiler_params=_cparams(("arbitrary", "arbitrary", "arbitrary")),
        name="inproj",
    )(xall, mod_l, g, w_big, w_small)


def _fwd_chunk(i, ncc, nch):
    return jnp.where(i < ncc, nch - ncc + i, i - ncc)


def _bwd_chunk(i, ncc, nch):
    del ncc
    return nch - 1 - i


def _full_spec(a):
    nd = a.ndim
    return pl.BlockSpec(a.shape, lambda b, i: (0,) * nd)


def _conv_kernel(xs_ref, xsp_ref, xsn_ref, xm_ref, xmp_ref, xmn_ref, cws_ref, cbs_ref, cwm_ref, cbm_ref,
                 us_ref, um_ref, *, lat_len, ta):
    tm = xs_ref.shape[1]
    row = pl.program_id(1) * tm + lax.broadcasted_iota(jnp.int32, (tm, 1), 0)
    no_prev = jnp.logical_or(row == 0, row == lat_len)
    no_next = jnp.logical_or(row == lat_len - 1, row == ta - 1)
    rid = lax.broadcasted_iota(jnp.int32, (tm, LANES), 0)
    for x_ref, xp_ref, xn_ref, cw_ref, cb_ref, o_ref in ((xs_ref, xsp_ref, xsn_ref, cws_ref, cbs_ref, us_ref),
                                                         (xm_ref, xmp_ref, xmn_ref, cwm_ref, cbm_ref, um_ref)):
        for j in range(x_ref.shape[2] // LANES):
            sl = slice(j * LANES, (j + 1) * LANES)
            x = x_ref[0, :, sl].astype(F32)
            prow = xp_ref[0, BF16_ROWS - 1:BF16_ROWS, sl].astype(F32)
            nrow = xn_ref[0, 0:1, sl].astype(F32)
            xprev = jnp.where(no_prev, 0.0, jnp.where(rid == 0, prow, pltpu.roll(x, 1, 0)))
            xnext = jnp.where(no_next, 0.0, jnp.where(rid == tm - 1, nrow, pltpu.roll(x, tm - 1, 0)))
            u = xprev * cw_ref[0:1, sl] + x * cw_ref[1:2, sl] + xnext * cw_ref[2:3, sl] + cb_ref[:, sl]
            o_ref[0, :, sl] = _silu(u).astype(BF16)


def _conv(p_big, cws, cbs, cwm, cbm, ctx_len):
    nb, ta, _ = p_big.shape
    tm = _row_tile(ta, 352)
    per = tm // BF16_ROWS
    nrow16 = ta // BF16_ROWS
    ws, wm = cws.shape[1], cwm.shape[1]

    def specs(width, col_block):
        return [pl.BlockSpec((1, tm, width), lambda b, i: (b, i, col_block)),
                pl.BlockSpec((1, BF16_ROWS, width), lambda b, i: (b, jnp.maximum(i * per - 1, 0), col_block)),
                pl.BlockSpec((1, BF16_ROWS, width),
                             lambda b, i: (b, jnp.minimum((i + 1) * per, nrow16 - 1), col_block))]

    consts = [cws, cbs, cwm, cbm]
    return pl.pallas_call(
        functools.partial(_conv_kernel, lat_len=ta - ctx_len, ta=ta),
        grid=(nb, ta // tm),
        in_specs=specs(ws, OFF_XBC // ws) + specs(wm, OFF_MQK // wm) + [_full_spec(a) for a in consts],
        out_specs=[pl.BlockSpec((1, tm, ws), lambda b, i: (b, i, 0)),
                   pl.BlockSpec((1, tm, wm), lambda b, i: (b, i, 0))],
        out_shape=[jax.ShapeDtypeStruct((nb, ta, ws), BF16), jax.ShapeDtypeStruct((nb, ta, wm), BF16)],
        compiler_params=_cparams(("arbitrary", "arbitrary")),
        name="conv_silu",
    )(p_big, p_big, p_big, p_big, p_big, p_big, *consts)


def _ssd_direction(d, u_ref, ps_ref, dtb_ref, a_ref, dsk_ref, tri_ref, trit_ref, e_ref, y_ref, s_ref):
    L = CHUNK
    hp = SSD_HEADS * SSD_HEAD_DIM
    gn = SSD_GROUPS * SSD_STATE
    gw = hp // SSD_GROUPS
    xs = u_ref[0, :, :hp].astype(F32)
    bm = u_ref[0, :, hp:hp + gn]
    cm = u_ref[0, :, hp + gn:]

    dt = _softplus(ps_ref[0] + dtb_ref[...])
    a = dt * a_ref[...]
    tri = tri_ref[d]
    acs = _dot_exact_lhs01(tri, a)
    acs_t = _dot_exact_rhs01(a.T, trit_ref[d])
    last = L - 1 if d == 0 else 0
    acs_last = acs[last:last + 1, :]
    dend = jnp.exp(acs_last - acs)
    eacs = jnp.exp(acs)

    e01 = e_ref[d]
    dt_e = _expand_rhs01(dt, e01)
    dtend_e = _expand_rhs01(dt * dend, e01)
    eacs_e = _expand_rhs01(eacs, e01)
    xd = xs * dt_e
    xde = (xs * dtend_e).astype(BF16)
    cd_e = eacs_e[last:last + 1, :]

    mask = tri > 0
    lane = lax.broadcasted_iota(jnp.int32, (L, LANES), 1)
    left = lane < SSD_HEAD_DIM
    heads_per_group = SSD_HEADS // SSD_GROUPS
    ys = []
    for g in range(SSD_GROUPS):
        cg = cm[:, g * SSD_STATE:(g + 1) * SSD_STATE]
        bg = bm[:, g * SSD_STATE:(g + 1) * SSD_STATE]
        sg = _dot_nt(cg, bg)
        s_prev = s_ref[d, g]
        y_off = _dot(cg, s_prev.astype(BF16)) * eacs_e[:, g * gw:(g + 1) * gw]
        pairs = []
        for k in range(heads_per_group // 2):
            h0 = g * heads_per_group + 2 * k
            ms = []
            for h in (h0, h0 + 1):
                j = SSD_HEADS * d + h
                seg = acs[:, j:j + 1] - acs_t[j:j + 1, :]
                ms.append((sg * jnp.exp(jnp.where(mask, seg, -jnp.inf))).astype(BF16))
            xp = xd[:, h0 * SSD_HEAD_DIM:(h0 + 2) * SSD_HEAD_DIM]
            rhs = jnp.concatenate([jnp.where(left, xp, 0.0), jnp.where(left, 0.0, xp)], axis=0).astype(BF16)
            pairs.append(_dot(jnp.concatenate(ms, axis=1), rhs))
        ys.append(jnp.concatenate(pairs, axis=1) + y_off)
        s_ref[d, g] = s_prev * cd_e[:, g * gw:(g + 1) * gw] + _dot_tn(bg, xde[:, g * gw:(g + 1) * gw])
    y = jnp.concatenate(ys, axis=1)
    if d == 0:
        y = y + dsk_ref[...] * xs
    y_ref[0, 0] = y.astype(BF16)


def _ssd_kernel(uf, ub, psf, psb, dtb, aneg, dsk, tri, trit, e01, yf, yb, s_ref):
    @pl.when(pl.program_id(1) == 0)
    def _():
        s_ref[...] = jnp.zeros_like(s_ref)

    _ssd_direction(0, uf, psf, dtb, aneg, dsk, tri, trit, e01, yf, s_ref)
    _ssd_direction(1, ub, psb, dtb, aneg, dsk, tri, trit, e01, yb, s_ref)


def _ssd(u_ssd, p_small, dtb, aneg, dsk, tri, trit, e01, ctx_len):
    nb, ta, width = u_ssd.shape
    ncc, nch = ctx_len // CHUNK, ta // CHUNK
    hp = SSD_HEADS * SSD_HEAD_DIM
    xspecs = [pl.BlockSpec((1, CHUNK, width), lambda b, i: (b, _fwd_chunk(i, ncc, nch), 0)),
              pl.BlockSpec((1, CHUNK, width), lambda b, i: (b, _bwd_chunk(i, ncc, nch), 0))]
    ps_f = pl.BlockSpec((1, CHUNK, LANES), lambda b, i: (b, _fwd_chunk(i, ncc, nch), 0))
    ps_b = pl.BlockSpec((1, CHUNK, LANES), lambda b, i: (b, _bwd_chunk(i, ncc, nch), 0))
    consts = [dtb, aneg, dsk, tri, trit, e01]
    yf, yb = pl.pallas_call(
        _ssd_kernel,
        grid=(nb, nch),
        in_specs=xspecs + [ps_f, ps_b] + [_full_spec(a) for a in consts],
        out_specs=[pl.BlockSpec((1, 1, CHUNK, hp), lambda b, i: (0, b, _fwd_chunk(i, ncc, nch), 0)),
                   pl.BlockSpec((1, 1, CHUNK, hp), lambda b, i: (0, b, _bwd_chunk(i, ncc, nch), 0))],
        out_shape=[jax.ShapeDtypeStruct((1, nb, ta, hp), BF16)] * 2,
        scratch_shapes=[pltpu.VMEM((2, SSD_GROUPS, SSD_STATE, hp // SSD_GROUPS), F32)],
        compiler_params=_cparams(("arbitrary", "arbitrary")),
        name="ssd_scan",
    )(u_ssd, u_ssd, p_small, p_small, *consts)
    return yf[0], yb[0]


def _ml_direction(d, u_ref, v_ref, ps_ref, ib_ref, fb_ref, tri_ref, trit_ref, h_ref, c_ref, n_ref, m_ref):
    L = CHUNK
    hq = ML_HEADS * ML_QK_DIM
    u = u_ref[0].astype(F32)
    ps = ps_ref[0]
    li = ps + ib_ref[...]
    lf = -_softplus(-(ps + fb_ref[...]))
    tri = tri_ref[d]
    trit = trit_ref[d]
    bcum = _dot_exact_lhs01(tri, lf)
    bcum_t = _dot_exact_rhs01(lf.T, trit)
    li_t = li.T
    mask = tri > 0
    last = L - 1 if d == 0 else 0
    outs = []
    for h in range(ML_HEADS):
        ji = SM_IG + ML_HEADS * d + h
        jf = SM_FG + ML_HEADS * d + h
        q = u[:, h * ML_QK_DIM:(h + 1) * ML_QK_DIM]
        k = u[:, hq + h * ML_QK_DIM:hq + (h + 1) * ML_QK_DIM] * (ML_QK_DIM ** -0.5)
        v = v_ref[0][:, h * ML_V_DIM:(h + 1) * ML_V_DIM]
        qb = q.astype(BF16)
        kb = k.astype(BF16)
        bc_col = bcum[:, jf:jf + 1]
        bc_row = bcum_t[jf:jf + 1, :]
        li_col = li[:, ji:ji + 1]
        li_row = li_t[ji:ji + 1, :]
        b_last = bc_col[last:last + 1, :]
        c_prev = c_ref[d, h]
        n_prev = n_ref[d, h]
        m_prev = m_ref[d, h][:, 0:1]

        dm = jnp.where(mask, bc_col - bc_row + li_row, -jnp.inf)
        e = bc_col + m_prev
        m_t = jnp.maximum(e, jnp.max(dm, axis=-1, keepdims=True))
        wts = jnp.exp(dm - m_t) * _dot_nt(qb, kb)
        sc = jnp.exp(e - m_t)
        num = _dot(wts.astype(BF16), v) + _dot(qb, c_prev.astype(BF16)) * sc
        den = jnp.sum(wts, axis=-1, keepdims=True) + jnp.sum(q * n_prev, axis=-1, keepdims=True) * sc
        den = jnp.maximum(jnp.abs(den), jnp.exp(-m_t))
        outs.append(num / den)

        gl = b_last - bc_col + li_col
        m_loc = jnp.max(gl, axis=0, keepdims=True)
        w = jnp.exp(gl - m_loc)
        kw = k * w
        c_loc = _dot_tn(kw.astype(BF16), v)
        n_loc = jnp.sum(kw, axis=0, keepdims=True)
        m_new = jnp.maximum(b_last + m_prev, m_loc)
        a_s = jnp.exp(b_last + m_prev - m_new)
        s_s = jnp.exp(m_loc - m_new)
        c_ref[d, h] = a_s * c_prev + s_s * c_loc
        n_ref[d, h] = a_s * n_prev + s_s * n_loc
        m_ref[d, h] = jnp.broadcast_to(m_new, (1, LANES))
    h_ref[0, 0] = jnp.concatenate(outs, axis=1).astype(BF16)


def _ml_kernel(uf, ub, vf, vb, psf, psb, ib, fb, tri, trit, hf, hb, c_ref, n_ref, m_ref):
    @pl.when(pl.program_id(1) == 0)
    def _():
        c_ref[...] = jnp.zeros_like(c_ref)
        n_ref[...] = jnp.zeros_like(n_ref)
        m_ref[...] = jnp.zeros_like(m_ref)

    _ml_direction(0, uf, vf, psf, ib, fb, tri, trit, hf, c_ref, n_ref, m_ref)
    _ml_direction(1, ub, vb, psb, ib, fb, tri, trit, hb, c_ref, n_ref, m_ref)


def _mlstm(u_ml, p_big, p_small, ib, fb, tri, trit, ctx_len):
    nb, ta, wq = u_ml.shape
    ncc, nch = ctx_len // CHUNK, ta // CHUNK
    wv = ML_HEADS * ML_V_DIM
    qspecs = [pl.BlockSpec((1, CHUNK, wq), lambda b, i: (b, _fwd_chunk(i, ncc, nch), 0)),
              pl.BlockSpec((1, CHUNK, wq), lambda b, i: (b, _bwd_chunk(i, ncc, nch), 0))]
    v_f = pl.BlockSpec((1, CHUNK, wv), lambda b, i: (b, _fwd_chunk(i, ncc, nch), OFF_MV // wv))
    v_b = pl.BlockSpec((1, CHUNK, wv), lambda b, i: (b, _bwd_chunk(i, ncc, nch), OFF_MV // wv))
    ps_f = pl.BlockSpec((1, CHUNK, LANES), lambda b, i: (b, _fwd_chunk(i, ncc, nch), 0))
    ps_b = pl.BlockSpec((1, CHUNK, LANES), lambda b, i: (b, _bwd_chunk(i, ncc, nch), 0))
    consts = [ib, fb, tri, trit]
    hf, hb = pl.pallas_call(
        _ml_kernel,
        grid=(nb, nch),
        in_specs=qspecs + [v_f, v_b, ps_f, ps_b] + [_full_spec(a) for a in consts],
        out_specs=[pl.BlockSpec((1, 1, CHUNK, wv), lambda b, i: (0, b, _fwd_chunk(i, ncc, nch), 0)),
                   pl.BlockSpec((1, 1, CHUNK, wv), lambda b, i: (0, b, _bwd_chunk(i, ncc, nch), 0))],
        out_shape=[jax.ShapeDtypeStruct((1, nb, ta, wv), BF16)] * 2,
        scratch_shapes=[pltpu.VMEM((2, ML_HEADS, ML_QK_DIM, ML_V_DIM), F32),
                        pltpu.VMEM((2, ML_HEADS, 1, ML_QK_DIM), F32),
                        pltpu.VMEM((2, ML_HEADS, 1, LANES), F32)],
        compiler_params=_cparams(("arbitrary", "arbitrary")),
        name="mlstm_scan",
    )(u_ml, u_ml, p_big, p_big, p_small, p_small, *consts)
    return hf[0], hb[0]


def _qk_prep_kernel(q_ref, k_ref, cos_ref, sa_ref, sb_ref, gq_ref, gk_ref, gm_ref, qo_ref, ko_ref, *, qscale):
    gm = gm_ref[...]
    nheads = q_ref.shape[2] // LANES
    tm = q_ref.shape[1]
    rb = _row_tile(tm, 192)

    def prep(t, g, mult, cos, sa, sb):
        t = t.astype(F32)
        sq = t * t
        sh = sq.astype(BF16)
        sm = (sq - sh.astype(F32)).astype(BF16)
        ms = (_dot(sh, gm) + _dot(sm, gm)) * (1.0 / DIFF_HEAD_DIM)
        y = t * lax.rsqrt(ms + EPS) * g
        r = y * cos + pltpu.roll(y, LANES - 16, 1) * sa + pltpu.roll(y, 16, 1) * sb
        return (r * mult).astype(BF16)

    for r in range(tm // rb):
        rows = slice(r * rb, (r + 1) * rb)
        tabs = (cos_ref[rows, :], sa_ref[rows, :], sb_ref[rows, :])
        for h in range(nheads):
            sl = slice(h * LANES, (h + 1) * LANES)
            qo_ref[0, rows, sl] = prep(q_ref[0, rows, sl], gq_ref[...], qscale, *tabs)
            ko_ref[0, rows, sl] = prep(k_ref[0, rows, sl], gk_ref[...], 1.0, *tabs)


def _qk_prep(p_big, cos, sa, sb, gq, gk, gm):
    nb, ta, _ = p_big.shape
    w = DIFF_HEADS * 2 * DIFF_HEAD_DIM
    tm = _row_tile(ta, 1056)
    qscale = DIFF_HEAD_DIM ** -0.5 * math.log2(math.e)
    row = lambda b, i: (i, 0)
    return pl.pallas_call(
        functools.partial(_qk_prep_kernel, qscale=qscale),
        grid=(nb, ta // tm),
        in_specs=[pl.BlockSpec((1, tm, w), lambda b, i: (b, i, OFF_Q // w)),
                  pl.BlockSpec((1, tm, w), lambda b, i: (b, i, OFF_K // w)),
                  pl.BlockSpec((tm, LANES), row), pl.BlockSpec((tm, LANES), row), pl.BlockSpec((tm, LANES), row),
                  _full_spec(gq), _full_spec(gk), _full_spec(gm)],
        out_specs=[pl.BlockSpec((1, tm, w), lambda b, i: (b, i, 0))] * 2,
        out_shape=[jax.ShapeDtypeStruct((nb, ta, w), BF16)] * 2,
        compiler_params=_cparams(("arbitrary", "arbitrary")),
        name="qk_prep",
    )(p_big, p_big, cos, sa, sb, gq, gk, gm)


def _attn_kernel(q_ref, k_ref, v_ref, z_ref, lam_ref, gq_ref, gk_ref, sg_ref, *rest, tk, lam_init):
    o_ref, e_ref, l_ref, r_ref, qq_ref = rest[-5:]
    nkeys = k_ref.shape[1]
    dh = DIFF_HEAD_DIM

    lp = lam_ref[...]
    lam = (jnp.exp(jnp.sum(lp[0:1] * lp[1:2], axis=-1, keepdims=True))
           - jnp.exp(jnp.sum(lp[2:3] * lp[3:4], axis=-1, keepdims=True)) + lam_init)
    shift = (jnp.max(jnp.abs(gq_ref[...]), axis=-1, keepdims=True)
             * jnp.max(jnp.abs(gk_ref[...]), axis=-1, keepdims=True)
             * (dh * dh ** -0.5 * math.log2(math.e)))

    tq = q_ref.shape[1]
    tsub = e_ref.shape[2]
    nsub = tq // tsub
    nchunk = nkeys // tk
    lane = lax.broadcasted_iota(jnp.int32, (tsub, LANES), 1)

    def scores_chunk(t, j):
        par = t % 2
        e = jnp.exp2(_dot_nt(qq_ref[t], k_ref[0, j * tk:(j + 1) * tk, :]) - shift)
        part = e[:, :LANES]
        for u in range(1, tk // LANES):
            part = part + e[:, u * LANES:(u + 1) * LANES]
        for c in range(2):
            pc = part[c * tsub:(c + 1) * tsub]
            l_ref[par, c] = pc if j == 0 else l_ref[par, c] + pc
            e_ref[par, c, :, j * tk:(j + 1) * tk] = e[c * tsub:(c + 1) * tsub].astype(BF16)

    def normalisers(t):
        par = t % 2
        l0 = jnp.sum(l_ref[par, 0], axis=-1, keepdims=True)
        l1 = jnp.sum(l_ref[par, 1], axis=-1, keepdims=True)
        rho = lam * l0 / l1
        rho_hi = rho.astype(BF16)
        r_ref[par, 0] = jnp.broadcast_to(rho_hi, (tsub, LANES))
        r_ref[par, 1] = jnp.broadcast_to((rho - rho_hi.astype(F32)).astype(BF16), (tsub, LANES))
        return l0

    def combine_chunk(t, j):
        par = t % 2
        for u in range(tk // LANES):
            sl = slice(j * tk + u * LANES, j * tk + (u + 1) * LANES)
            e1 = e_ref[par, 1, :, sl]
            e_ref[par, 0, :, sl] = e_ref[par, 0, :, sl] - (e1 * r_ref[par, 0] + e1 * r_ref[par, 1])

    def read_out(t, l0):
        par = t % 2
        rows = slice(t * tsub, (t + 1) * tsub)
        o = _dot(e_ref[par, 0], v_ref[0]) / l0
        ms = jnp.mean(o * o, axis=-1, keepdims=True)
        o = o * lax.rsqrt(ms + EPS) * sg_ref[...] * (1.0 - lam_init)
        o_ref[0, rows, :] = (o * _silu(z_ref[0, rows, :].astype(F32))).astype(BF16)

    for t in range(nsub):
        q = q_ref[0, t * tsub:(t + 1) * tsub, :]
        zero = jnp.zeros_like(q)
        qq_ref[t] = jnp.concatenate([jnp.where(lane < dh, q, zero), jnp.where(lane < dh, zero, q)], axis=0)

    for j in range(nchunk):
        scores_chunk(0, j)
    for t in range(1, nsub):
        l0 = normalisers(t - 1)
        for j in range(nchunk):
            scores_chunk(t, j)
            combine_chunk(t - 1, j)
        read_out(t - 1, l0)
    l0 = normalisers(nsub - 1)
    for j in range(nchunk):
        combine_chunk(nsub - 1, j)
    read_out(nsub - 1, l0)


def _attention(qn, kn, p_big, lam_p, gq, gk, sg, ctx_len, lam_init):
    nb, ta, w = qn.shape
    seq = ta - ctx_len
    tk = 256
    tsub = 256
    tq = 1024 if seq % 1024 == 0 else 512
    assert seq % tq == 0 and seq % ctx_len == 0 and ta % tk == 0 and ctx_len % tk == 0 and ctx_len == tsub
    kern = functools.partial(_attn_kernel, tk=tk, lam_init=lam_init)
    full = lambda a: pl.BlockSpec(a.shape, lambda b, h, i: (0,) * a.ndim)
    consts = [lam_p, gq, gk, sg]

    def call(tq_, nkeys, row_block, key_block, nq, prev):
        in_specs = [pl.BlockSpec((1, tq_, LANES), lambda b, h, i: (b, row_block + i, h)),
                    pl.BlockSpec((1, nkeys, LANES), lambda b, h, i: (b, key_block, h)),
                    pl.BlockSpec((1, nkeys, LANES), lambda b, h, i: (b, key_block, OFF_V // LANES + h)),
                    pl.BlockSpec((1, tq_, LANES), lambda b, h, i: (b, row_block + i, OFF_ZD // LANES + h))]
        in_specs += [full(a) for a in consts]
        args = [qn, kn, p_big, p_big] + consts
        aliases = {}
        if prev is not None:
            in_specs.append(pl.BlockSpec(memory_space=pl.ANY))
            aliases = {len(args): 0}
            args.append(prev)
        return pl.pallas_call(
            kern,
            grid=(nb, DIFF_HEADS, nq),
            in_specs=in_specs,
            out_specs=pl.BlockSpec((1, tq_, LANES), lambda b, h, i: (b, row_block + i, h)),
            out_shape=jax.ShapeDtypeStruct((nb, ta, w), BF16),
            scratch_shapes=[pltpu.VMEM((2, 2, tsub, nkeys), BF16), pltpu.VMEM((2, 2, tsub, LANES), F32),
                            pltpu.VMEM((2, 2, tsub, LANES), BF16), pltpu.VMEM((tq_ // tsub, 2 * tsub, LANES), BF16)],
            input_output_aliases=aliases,
            compiler_params=_cparams(("arbitrary", "arbitrary", "arbitrary")),
            name="diff_attention" if prev is None else "diff_attention_ctx",
        )(*args)

    yd = call(tq, ta, 0, 0, seq // tq, None)
    return call(ctx_len, ctx_len, seq // ctx_len, seq // ctx_len, 1, yd)


def _merge_kernel(x_ref, mod_ref, yf_ref, yb_ref, zs_ref, yd_ref, hf_ref, hb_ref, mo_ref, mz_ref, gt_ref,
                  sng_ref, mng_ref, wb_ref, wo_ref, o_ref, *, lat_len, ctx_row):
    b, i = pl.program_id(0), pl.program_id(1)
    tm, d = x_ref.shape[1], x_ref.shape[2]

    ya = (yf_ref[0].astype(F32) + yb_ref[0].astype(F32)) * _silu(zs_ref[0].astype(F32))
    ya = ya * lax.rsqrt(jnp.mean(ya * ya, axis=-1, keepdims=True) + EPS) * sng_ref[...]

    hm = (hf_ref[0].astype(F32) + hb_ref[0].astype(F32)) * _sigmoid(mo_ref[0].astype(F32))
    parts = []
    for h in range(ML_HEADS):
        t = hm[:, h * ML_V_DIM:(h + 1) * ML_V_DIM]
        parts.append(t * lax.rsqrt(jnp.mean(t * t, axis=-1, keepdims=True) + EPS))
    yc = jnp.concatenate(parts, axis=1) * mng_ref[...] * _silu(mz_ref[0].astype(F32))

    g = _sigmoid(gt_ref[0].astype(F32))
    mixed = (g[:, :d] * _dot(ya.astype(BF16), wb_ref[0])
             + g[:, d:2 * d] * _dot(yd_ref[0], wb_ref[1])
             + g[:, 2 * d:] * _dot(yc.astype(BF16), wb_ref[2]))
    out = _dot(mixed.astype(BF16), wo_ref[...])

    row = i * tm + lax.broadcasted_iota(jnp.int32, (tm, 1), 0)
    is_ctx = row >= lat_len
    gate = jnp.where(is_ctx, mod_ref[pl.ds(ctx_row, 1), 2 * d:], mod_ref[pl.ds(b, 1), 2 * d:])
    o_ref[0] = x_ref[0] + gate * out


def _merge(xall, mod_l, yf, yb, p_big, yd, hf, hb, sng, mng, wb, wo, ctx_len):
    nb, ta, d = xall.shape
    tm = _row_tile(ta, 352)
    kern = functools.partial(_merge_kernel, lat_len=ta - ctx_len, ctx_row=nb)
    blk = lambda col: pl.BlockSpec((1, tm, d), lambda b, i: (b, i, col))
    full = lambda a: pl.BlockSpec(a.shape, lambda b, i: (0,) * a.ndim)
    return pl.pallas_call(
        kern,
        grid=(nb, ta // tm),
        in_specs=[blk(0), full(mod_l), blk(0), blk(0), blk(OFF_ZS // d), blk(0), blk(0), blk(0),
                  blk(OFF_MO // d), blk(OFF_MZ // d),
                  pl.BlockSpec((1, tm, 3 * d), lambda b, i: (b, i, OFF_GATES // (3 * d))),
                  full(sng), full(mng), full(wb), full(wo)],
        out_specs=blk(0),
        out_shape=jax.ShapeDtypeStruct((nb, ta, d), F32),
        compiler_params=_cparams(("arbitrary", "arbitrary")),
        name="merge",
    )(xall, mod_l, yf, yb, p_big, yd, hf, hb, p_big, p_big, p_big, sng, mng, wb, wo)


def _rope_tables(ctx_len, seq):
    rows = seq // GRID_W
    row = jnp.repeat(jnp.arange(rows, dtype=F32), GRID_W)
    col = jnp.tile(jnp.arange(GRID_W, dtype=F32), rows)
    half = DIFF_HEAD_DIM // 2
    inv_freq = ROPE_BASE ** (-jnp.arange(0, half, 2, dtype=F32) / half)
    ang_r = row[:, None] * inv_freq
    ang_c = col[:, None] * inv_freq
    cos = jnp.concatenate([jnp.cos(ang_r), jnp.cos(ang_r), jnp.cos(ang_c), jnp.cos(ang_c)], axis=-1)
    sin = jnp.concatenate([jnp.sin(ang_r), jnp.sin(ang_r), jnp.sin(ang_c), jnp.sin(ang_c)], axis=-1)
    cos = jnp.concatenate([cos, jnp.ones((ctx_len, DIFF_HEAD_DIM), F32)], axis=0)
    sin = jnp.concatenate([sin, jnp.zeros((ctx_len, DIFF_HEAD_DIM), F32)], axis=0)
    cos = jnp.tile(cos, (1, LANES // DIFF_HEAD_DIM))
    sin = jnp.tile(sin, (1, LANES // DIFF_HEAD_DIM))
    first = (jnp.arange(LANES) % (DIFF_HEAD_DIM // 2)) < DIFF_HEAD_DIM // 4
    return cos, jnp.where(first, -sin, 0.0), jnp.where(first, 0.0, sin)


def _scan_constants():
    idx = np.arange(CHUNK)
    tri_f = (idx[None, :] <= idx[:, None]).astype(np.float32)
    tri = np.stack([tri_f, tri_f.T])
    trit = np.stack([tri_f.T, tri_f])
    e01 = np.zeros((2, LANES, SSD_HEADS * SSD_HEAD_DIM), np.float32)
    for d in range(2):
        for h in range(SSD_HEADS):
            e01[d, SSD_HEADS * d + h, h * SSD_HEAD_DIM:(h + 1) * SSD_HEAD_DIM] = 1.0
    half = np.arange(LANES) // DIFF_HEAD_DIM
    gm = (half[:, None] == half[None, :]).astype(np.float32)
    return (jnp.asarray(tri, BF16), jnp.asarray(trit, BF16), jnp.asarray(e01, BF16), jnp.asarray(gm, BF16))


def _pad_row(v, offset):
    v = v.reshape(1, -1).astype(F32)
    return jnp.pad(v, ((0, 0), (offset, LANES - offset - v.shape[1])))


def kernel(x, c, ctx, c_ctx, w_mod, b_mod, norm_g, w_in, ssd_conv_w, ssd_conv_b, ssd_a_log, ssd_dt_bias, ssd_d,
           ssd_norm_g, diff_qn_g, diff_kn_g, diff_lambda, diff_subln_g, ml_conv_w, ml_conv_b, ml_i_bias,
           ml_f_bias, ml_norm_g, w_branch, w_out):
    nb, seq, d = x.shape
    ctx_len = ctx.shape[1]
    depth = w_mod.shape[0]
    assert d == 1024 and ctx_len % CHUNK == 0 and seq % CHUNK == 0 and seq % GRID_W == 0 and nb < 8

    sizes = (1536, 1024, 32, 1024, 1024, 1024, 1024, 1024, 1024, 1024, 1024, 8, 8, 3072)
    offs = np.concatenate([[0], np.cumsum(sizes)])
    seg = lambda n: w_in[:, :, offs[n]:offs[n + 1]]
    w_big = jnp.concatenate([seg(1), seg(3), seg(4), seg(5), seg(6), seg(7), seg(8), seg(9), seg(10),
                             seg(13), seg(0)], axis=-1).astype(BF16)
    w_small = jnp.concatenate([seg(2), seg(11), seg(12)], axis=-1)
    w_small = jnp.pad(w_small, ((0, 0), (0, 0), (0, LANES - w_small.shape[-1]))).astype(BF16)

    cc = jnp.concatenate([c, c_ctx[None, :], jnp.zeros((8 - nb - 1, d), F32)], axis=0)
    mod = _modulation(cc, w_mod, b_mod)

    tri, trit, e01, gm = _scan_constants()
    cos, sa, sb = _rope_tables(ctx_len, seq)
    wb = w_branch.astype(BF16)
    wo = w_out.astype(BF16)
    xall = jnp.concatenate([x, ctx], axis=1)

    for l in range(depth):
        lam_init = 0.8 - 0.6 * math.exp(-0.3 * l)
        p_big, p_small = _inproj(xall, mod[l], norm_g[l][None, :], w_big[l], w_small[l], ctx_len)
        u_ssd, u_ml = _conv(p_big, ssd_conv_w[l], ssd_conv_b[l][None, :], ml_conv_w[l], ml_conv_b[l][None, :],
                            ctx_len)
        yf, yb = _ssd(u_ssd, p_small, _pad_row(ssd_dt_bias[l], 0), _pad_row(-jnp.exp(ssd_a_log[l]), 0),
                      jnp.repeat(ssd_d[l], SSD_HEAD_DIM)[None, :], tri, trit, e01, ctx_len)
        qn, kn = _qk_prep(p_big, cos, sa, sb, jnp.tile(diff_qn_g[l], 2)[None, :],
                          jnp.tile(diff_kn_g[l], 2)[None, :], gm)
        yd = _attention(qn, kn, p_big, diff_lambda[l], diff_qn_g[l][None, :], diff_kn_g[l][None, :],
                        diff_subln_g[l][None, :], ctx_len, lam_init)
        hf, hb = _mlstm(u_ml, p_big, p_small, _pad_row(ml_i_bias[l], SM_IG), _pad_row(ml_f_bias[l], SM_FG),
                        tri, trit, ctx_len)
        xall = _merge(xall, mod[l], yf, yb, p_big, yd, hf, hb, ssd_norm_g[l][None, :], ml_norm_g[l][None, :],
                      wb[l], wo[l], ctx_len)
    return xall[:, :seq]
```

```python
import functools
import math

import numpy as np
import jax
import jax.numpy as jnp
from jax import lax
from jax.experimental import pallas as pl
from jax.experimental.pallas import tpu as pltpu

F32 = jnp.float32
BF16 = jnp.bfloat16

GRID_W = 64
EPS = 1e-6
CONV_K = 3
SSD_HEADS = 16
SSD_HEAD_DIM = 64
SSD_GROUPS = 2
SSD_STATE = 128
DIFF_HEADS = 8
DIFF_HEAD_DIM = 64
ROPE_BASE = 10000.0
ML_HEADS = 4
ML_QK_DIM = 128
ML_V_DIM = 256
CHUNK = 128
LANES = 128
BF16_ROWS = 16

OFF_ZS, OFF_Q, OFF_K, OFF_V, OFF_ZD = 0, 1024, 2048, 3072, 4096
OFF_MQK, OFF_MV, OFF_MO, OFF_MZ, OFF_GATES, OFF_XBC = 5120, 6144, 7168, 8192, 9216, 12288
N_BIG = 13824
SM_IG, SM_FG = 32, 40

VMEM_LIMIT = 56 * 1024 * 1024


def _cparams(sem):
    return pltpu.CompilerParams(dimension_semantics=sem, vmem_limit_bytes=VMEM_LIMIT)


def _split3(v):
    h = v.astype(BF16)
    r = v - h.astype(F32)
    m = r.astype(BF16)
    l = (r - m.astype(F32)).astype(BF16)
    return h, m, l


def _dot(a, b):
    return jnp.dot(a, b, preferred_element_type=F32)


def _dot_nt(a, b):
    return lax.dot_general(a, b, (((1,), (1,)), ((), ())), preferred_element_type=F32)


def _dot_tn(a, b):
    return lax.dot_general(a, b, (((0,), (0,)), ((), ())), preferred_element_type=F32)


def _dot_exact_rhs01(v, m01):
    h, m, l = _split3(v)
    return _dot(h, m01) + _dot(m, m01) + _dot(l, m01)


def _expand_rhs01(v, m01):
    h = v.astype(BF16)
    m = (v - h.astype(F32)).astype(BF16)
    return _dot(h, m01) + _dot(m, m01)


def _dot_exact_lhs01(m01, v):
    h, m, l = _split3(v)
    return _dot(m01, h) + _dot(m01, m) + _dot(m01, l)


def _sigmoid(x):
    return 1.0 / (1.0 + jnp.exp(-x))


def _silu(x):
    return x * _sigmoid(x)


def _softplus(x):
    return jnp.maximum(x, 0.0) + jnp.log(1.0 + jnp.exp(-jnp.abs(x)))


def _mod_kernel(cc_ref, w_ref, b_ref, o_ref):
    a = _silu(cc_ref[...])
    w = w_ref[0]
    ah, am, _ = _split3(a)
    wh = w.astype(BF16)
    wm = (w - wh.astype(F32)).astype(BF16)
    o_ref[0] = _dot(ah, wh) + _dot(am, wh) + _dot(ah, wm) + b_ref[0]


def _modulation(cc, w_mod, b_mod):
    depth, d, d3 = w_mod.shape
    tn = 1024
    return pl.pallas_call(
        _mod_kernel,
        grid=(depth, d3 // tn),
        in_specs=[pl.BlockSpec((8, d), lambda l, j: (0, 0)),
                  pl.BlockSpec((1, d, tn), lambda l, j: (l, 0, j)),
                  pl.BlockSpec((1, 1, tn), lambda l, j: (l, 0, j))],
        out_specs=pl.BlockSpec((1, 8, tn), lambda l, j: (l, 0, j)),
        out_shape=jax.ShapeDtypeStruct((depth, 8, d3), F32),
        compiler_params=_cparams(("arbitrary", "arbitrary")),
        name="modulation",
    )(cc, w_mod, b_mod.reshape(depth, 1, d3))


def _inproj_kernel(x_ref, mod_ref, g_ref, w_ref, ws_ref, o_ref, os_ref, h_ref, *, lat_len, ctx_row):
    b, i, j = pl.program_id(0), pl.program_id(1), pl.program_id(2)
    tm, d = h_ref.shape

    @pl.when(j == 0)
    def _():
        x = x_ref[0]
        ms = jnp.mean(x * x, axis=-1, keepdims=True)
        y = x * lax.rsqrt(ms + EPS) * g_ref[...]
        row = i * tm + lax.broadcasted_iota(jnp.int32, (tm, 1), 0)
        is_ctx = row >= lat_len
        mlat = mod_ref[pl.ds(b, 1), :]
        mctx = mod_ref[pl.ds(ctx_row, 1), :]
        shift = jnp.where(is_ctx, mctx[:, :d], mlat[:, :d])
        scale = jnp.where(is_ctx, mctx[:, d:2 * d], mlat[:, d:2 * d])
        h = (y * (1.0 + scale) + shift).astype(BF16)
        h_ref[...] = h
        os_ref[0] = _dot(h, ws_ref[...])

    o_ref[0] = _dot(h_ref[...], w_ref[...]).astype(BF16)


def _row_tile(ta, target):
    best = BF16_ROWS
    for t in range(BF16_ROWS, target + 1, BF16_ROWS):
        if ta % t == 0:
            best = t
    return best


def _inproj(xall, mod_l, g, w_big, w_small, ctx_len):
    nb, ta, d = xall.shape
    tm = _row_tile(ta, 1056)
    tn = 1536
    kern = functools.partial(_inproj_kernel, lat_len=ta - ctx_len, ctx_row=nb)
    return pl.pallas_call(
        kern,
        grid=(nb, ta // tm, N_BIG // tn),
        in_specs=[pl.BlockSpec((1, tm, d), lambda b, i, j: (b, i, 0)),
                  pl.BlockSpec(mod_l.shape, lambda b, i, j: (0, 0)),
                  pl.BlockSpec((1, d), lambda b, i, j: (0, 0)),
                  pl.BlockSpec((d, tn), lambda b, i, j: (0, j)),
                  pl.BlockSpec((d, LANES), lambda b, i, j: (0, 0))],
        out_specs=[pl.BlockSpec((1, tm, tn), lambda b, i, j: (b, i, j)),
                   pl.BlockSpec((1, tm, LANES), lambda b, i, j: (b, i, 0))],
        out_shape=[jax.ShapeDtypeStruct((nb, ta, N_BIG), BF16),
                   jax.ShapeDtypeStruct((nb, ta, LANES), F32)],
        scratch_shapes=[pltpu.VMEM((tm, d), BF16)],
        compiler_params=_cparams(("arbitrary", "arbitrary", "arbitrary")),
        name="inproj",
    )(xall, mod_l, g, w_big, w_small)


def _scan_batch(nb):
    return 4 if nb % 4 == 0 else (2 if nb % 2 == 0 else 1)


def _fwd_chunk(i, ncc, nch):
    return jnp.where(i < ncc, nch - ncc + i, i - ncc)


def _bwd_chunk(i, ncc, nch):
    del ncc
    return nch - 1 - i


def _full_spec(a):
    nd = a.ndim
    return pl.BlockSpec(a.shape, lambda b, i: (0,) * nd)


def _conv_kernel(xs_ref, xsp_ref, xsn_ref, xm_ref, xmp_ref, xmn_ref, cws_ref, cbs_ref, cwm_ref, cbm_ref,
                 us_ref, um_ref, *, lat_len, ta):
    tm = xs_ref.shape[1]
    row = pl.program_id(1) * tm + lax.broadcasted_iota(jnp.int32, (tm, 1), 0)
    no_prev = jnp.logical_or(row == 0, row == lat_len)
    no_next = jnp.logical_or(row == lat_len - 1, row == ta - 1)
    rid = lax.broadcasted_iota(jnp.int32, (tm, LANES), 0)
    for x_ref, xp_ref, xn_ref, cw_ref, cb_ref, o_ref in ((xs_ref, xsp_ref, xsn_ref, cws_ref, cbs_ref, us_ref),
                                                         (xm_ref, xmp_ref, xmn_ref, cwm_ref, cbm_ref, um_ref)):
        for j in range(x_ref.shape[2] // LANES):
            sl = slice(j * LANES, (j + 1) * LANES)
            x = x_ref[0, :, sl].astype(F32)
            prow = xp_ref[0, BF16_ROWS - 1:BF16_ROWS, sl].astype(F32)
            nrow = xn_ref[0, 0:1, sl].astype(F32)
            xprev = jnp.where(no_prev, 0.0, jnp.where(rid == 0, prow, pltpu.roll(x, 1, 0)))
            xnext = jnp.where(no_next, 0.0, jnp.where(rid == tm - 1, nrow, pltpu.roll(x, tm - 1, 0)))
            u = xprev * cw_ref[0:1, sl] + x * cw_ref[1:2, sl] + xnext * cw_ref[2:3, sl] + cb_ref[:, sl]
            o_ref[0, :, sl] = _silu(u).astype(BF16)


def _conv(p_big, cws, cbs, cwm, cbm, ctx_len):
    nb, ta, _ = p_big.shape
    tm = _row_tile(ta, 352)
    per = tm // BF16_ROWS
    nrow16 = ta // BF16_ROWS
    ws, wm = cws.shape[1], cwm.shape[1]

    def specs(width, col_block):
        return [pl.BlockSpec((1, tm, width), lambda b, i: (b, i, col_block)),
                pl.BlockSpec((1, BF16_ROWS, width), lambda b, i: (b, jnp.maximum(i * per - 1, 0), col_block)),
                pl.BlockSpec((1, BF16_ROWS, width),
                             lambda b, i: (b, jnp.minimum((i + 1) * per, nrow16 - 1), col_block))]

    consts = [cws, cbs, cwm, cbm]
    return pl.pallas_call(
        functools.partial(_conv_kernel, lat_len=ta - ctx_len, ta=ta),
        grid=(nb, ta // tm),
        in_specs=specs(ws, OFF_XBC // ws) + specs(wm, OFF_MQK // wm) + [_full_spec(a) for a in consts],
        out_specs=[pl.BlockSpec((1, tm, ws), lambda b, i: (b, i, 0)),
                   pl.BlockSpec((1, tm, wm), lambda b, i: (b, i, 0))],
        out_shape=[jax.ShapeDtypeStruct((nb, ta, ws), BF16), jax.ShapeDtypeStruct((nb, ta, wm), BF16)],
        compiler_params=_cparams(("arbitrary", "arbitrary")),
        name="conv_silu",
    )(p_big, p_big, p_big, p_big, p_big, p_big, *consts)


def _ssd_stats(d, ps, dtb_ref, a_ref, tri_ref, trit_ref, e_ref):
    L = CHUNK
    dt = _softplus(ps + dtb_ref[...])
    a = dt * a_ref[...]
    acs = _dot_exact_lhs01(tri_ref[d], a)
    acs_t = _dot_exact_rhs01(a.T, trit_ref[d])
    last = L - 1 if d == 0 else 0
    dend = jnp.exp(acs[last:last + 1, :] - acs)
    e01 = e_ref[d]
    eacs_e = _expand_rhs01(jnp.exp(acs), e01)
    return dict(acs=acs, acs_t=acs_t, dt_e=_expand_rhs01(dt, e01), dtend_e=_expand_rhs01(dt * dend, e01),
                eacs_e=eacs_e, cd_e=eacs_e[last:last + 1, :])


def _ssd_kernel(uf, ub, psf, psb, dtb, aneg, dsk, tri, trit, e01, yf, yb, s_ref):
    @pl.when(pl.program_id(1) == 0)
    def _():
        s_ref[...] = jnp.zeros_like(s_ref)

    L = CHUNK
    hp = SSD_HEADS * SSD_HEAD_DIM
    gn = SSD_GROUPS * SSD_STATE
    gw = hp // SSD_GROUPS
    hpg = SSD_HEADS // SSD_GROUPS
    u_refs, y_refs, ps_refs = (uf, ub), (yf, yb), (psf, psb)
    bds = [(b, d) for b in range(uf.shape[0]) for d in range(2)]
    dgs = [(b, d, g) for b, d in bds for g in range(SSD_GROUPS)]

    cg, bg, sg, yo = {}, {}, {}, {}
    for b, d, g in dgs:
        bg[b, d, g] = u_refs[d][b, :, hp + g * SSD_STATE:hp + (g + 1) * SSD_STATE]
        cg[b, d, g] = u_refs[d][b, :, hp + gn + g * SSD_STATE:hp + gn + (g + 1) * SSD_STATE]
        sg[b, d, g] = _dot_nt(cg[b, d, g], bg[b, d, g])
        yo[b, d, g] = _dot(cg[b, d, g], s_ref[b, d, g].astype(BF16))
    st = {(b, d): _ssd_stats(d, ps_refs[d][b], dtb, aneg, tri, trit, e01) for b, d in bds}
    xs = {(b, d): u_refs[d][b, :, :hp].astype(F32) for b, d in bds}
    masks = [tri[d] > 0 for d in range(2)]
    left = lax.broadcasted_iota(jnp.int32, (L, LANES), 1) < SSD_HEAD_DIM

    pairs, snew = {}, {}
    for b, d, g in dgs:
        sd = st[b, d]
        xd = xs[b, d][:, g * gw:(g + 1) * gw] * sd["dt_e"][:, g * gw:(g + 1) * gw]
        for k in range(hpg // 2):
            ms = []
            for h in (g * hpg + 2 * k, g * hpg + 2 * k + 1):
                j = SSD_HEADS * d + h
                seg = sd["acs"][:, j:j + 1] - sd["acs_t"][j:j + 1, :]
                ms.append((sg[b, d, g] * jnp.exp(jnp.where(masks[d], seg, -jnp.inf))).astype(BF16))
            xp = xd[:, 2 * k * SSD_HEAD_DIM:(2 * k + 2) * SSD_HEAD_DIM]
            rhs = jnp.concatenate([jnp.where(left, xp, 0.0), jnp.where(left, 0.0, xp)], axis=0).astype(BF16)
            pairs[b, d, g, k] = _dot(jnp.concatenate(ms, axis=1), rhs)
        xde = (xs[b, d][:, g * gw:(g + 1) * gw] * sd["dtend_e"][:, g * gw:(g + 1) * gw]).astype(BF16)
        snew[b, d, g] = _dot_tn(bg[b, d, g], xde)

    for b, d in bds:
        ys = []
        for g in range(SSD_GROUPS):
            sl = slice(g * gw, (g + 1) * gw)
            ys.append(jnp.concatenate([pairs[b, d, g, k] for k in range(hpg // 2)], axis=1)
                      + yo[b, d, g] * st[b, d]["eacs_e"][:, sl])
            s_ref[b, d, g] = s_ref[b, d, g] * st[b, d]["cd_e"][:, sl] + snew[b, d, g]
        y = jnp.concatenate(ys, axis=1)
        if d == 0:
            y = y + dsk[...] * xs[b, d]
        y_refs[d][0, b] = y.astype(BF16)


def _ssd(u_ssd, p_small, dtb, aneg, dsk, tri, trit, e01, ctx_len):
    nb, ta, width = u_ssd.shape
    ncc, nch = ctx_len // CHUNK, ta // CHUNK
    hp = SSD_HEADS * SSD_HEAD_DIM
    bb = _scan_batch(nb)
    xspecs = [pl.BlockSpec((bb, CHUNK, width), lambda b, i: (b, _fwd_chunk(i, ncc, nch), 0)),
              pl.BlockSpec((bb, CHUNK, width), lambda b, i: (b, _bwd_chunk(i, ncc, nch), 0))]
    ps_f = pl.BlockSpec((bb, CHUNK, LANES), lambda b, i: (b, _fwd_chunk(i, ncc, nch), 0))
    ps_b = pl.BlockSpec((bb, CHUNK, LANES), lambda b, i: (b, _bwd_chunk(i, ncc, nch), 0))
    consts = [dtb, aneg, dsk, tri, trit, e01]
    yf, yb = pl.pallas_call(
        _ssd_kernel,
        grid=(nb // bb, nch),
        in_specs=xspecs + [ps_f, ps_b] + [_full_spec(a) for a in consts],
        out_specs=[pl.BlockSpec((1, bb, CHUNK, hp), lambda b, i: (0, b, _fwd_chunk(i, ncc, nch), 0)),
                   pl.BlockSpec((1, bb, CHUNK, hp), lambda b, i: (0, b, _bwd_chunk(i, ncc, nch), 0))],
        out_shape=[jax.ShapeDtypeStruct((1, nb, ta, hp), BF16)] * 2,
        scratch_shapes=[pltpu.VMEM((bb, 2, SSD_GROUPS, SSD_STATE, hp // SSD_GROUPS), F32)],
        compiler_params=_cparams(("arbitrary", "arbitrary")),
        name="ssd_scan",
    )(u_ssd, u_ssd, p_small, p_small, *consts)
    return yf[0], yb[0]


def _ml_stats(d, ps, ib_ref, fb_ref, trit_ref, m_prev):
    L = CHUNK
    li_t = (ps + ib_ref[...]).T
    lf_t = (-_softplus(-(ps + fb_ref[...]))).T
    bc = _dot_exact_rhs01(lf_t, trit_ref[d])[SM_FG:SM_FG + 8, :]
    lir = li_t[SM_IG:SM_IG + 8, :]
    last = L - 1 if d == 0 else 0
    lane = lax.broadcasted_iota(jnp.int32, (8, L), 1)
    b_last = jnp.broadcast_to(bc[:, last:last + 1], (8, L))

    gl = b_last - bc + lir
    m_loc = jnp.broadcast_to(jnp.max(gl, axis=-1, keepdims=True), (8, L))
    m_new = jnp.maximum(b_last + m_prev, m_loc)

    g = lir - bc
    sh = 1
    while sh < L:
        if d == 0:
            g = jnp.maximum(g, jnp.where(lane >= sh, pltpu.roll(g, sh, 1), -jnp.inf))
        else:
            g = jnp.maximum(g, jnp.where(lane < L - sh, pltpu.roll(g, L - sh, 1), -jnp.inf))
        sh *= 2
    e = bc + m_prev
    m_t = jnp.maximum(e, bc + g)
    cols = jnp.concatenate([bc - m_t, jnp.exp(e - m_t), jnp.exp(-m_t), jnp.zeros((L - 24, L), F32)], axis=0).T
    return dict(w=jnp.exp(gl - m_loc), a_s=jnp.exp(b_last + m_prev - m_new), s_s=jnp.exp(m_loc - m_new),
                m_new=m_new, rowb=bc - lir, cols=cols)


def _ml_kernel(uf, ub, vf, vb, psf, psb, ib, fb, tri, trit, hf, hb, c_ref, n_ref, m_ref):
    @pl.when(pl.program_id(1) == 0)
    def _():
        c_ref[...] = jnp.zeros_like(c_ref)
        n_ref[...] = jnp.zeros_like(n_ref)
        m_ref[...] = jnp.zeros_like(m_ref)

    L = CHUNK
    hq = ML_HEADS * ML_QK_DIM
    rep = ML_V_DIM // LANES
    ones = jnp.ones((L, LANES), BF16)
    u_refs, v_refs, h_refs, ps_refs = (uf, ub), (vf, vb), (hf, hb), (psf, psb)
    masks = [tri[d] > 0 for d in range(2)]
    bds = [(b, d) for b in range(uf.shape[0]) for d in range(2)]
    units = [(b, d, h) for b, d in bds for h in range(ML_HEADS)]

    st1 = {}
    for b, d, h in units:
        q = u_refs[d][b, :, h * ML_QK_DIM:(h + 1) * ML_QK_DIM]
        k = u_refs[d][b, :, hq + h * ML_QK_DIM:hq + (h + 1) * ML_QK_DIM].astype(F32) * (ML_QK_DIM ** -0.5)
        st1[b, d, h] = dict(k=k, s=_dot_nt(q, k.astype(BF16)), qc=_dot(q, c_ref[b, d, h].astype(BF16)),
                            qn=_dot(q, n_ref[b, d, h].astype(BF16)))
    stats = {(b, d): _ml_stats(d, ps_refs[d][b], ib, fb, trit, m_ref[b, d]) for b, d in bds}
    for b, d, h in units:
        r = ML_HEADS * d + h
        v = v_refs[d][b, :, h * ML_V_DIM:(h + 1) * ML_V_DIM]
        kwt = (st1[b, d, h]["k"].T * stats[b, d]["w"][r:r + 1, :]).astype(BF16)
        st1[b, d, h].update(v=v, c_loc=_dot(kwt, v), n_loc=_dot(kwt, ones))
    st2 = {}
    for b, d, h in units:
        r = ML_HEADS * d + h
        cola = stats[b, d]["cols"][:, r:r + 1]
        x = jnp.exp(jnp.where(masks[d], cola - stats[b, d]["rowb"][r:r + 1, :], -jnp.inf))
        wts = (x * st1[b, d, h]["s"]).astype(BF16)
        st2[b, d, h] = (_dot(wts, st1[b, d, h]["v"]), _dot(wts, ones))
    for b, d in bds:
        outs = []
        for h in range(ML_HEADS):
            r = ML_HEADS * d + h
            s1 = st1[b, d, h]
            sc = stats[b, d]["cols"][:, 8 + r:9 + r]
            emt = stats[b, d]["cols"][:, 16 + r:17 + r]
            wv, wo = st2[b, d, h]
            num = wv + s1["qc"] * sc
            den = wo + s1["qn"] * sc
            rden = 1.0 / jnp.maximum(jnp.abs(den), emt)
            outs.append(num * jnp.concatenate([rden] * rep, axis=1))
            a_r = stats[b, d]["a_s"][r:r + 1, :]
            s_r = stats[b, d]["s_s"][r:r + 1, :]
            c_ref[b, d, h] = (jnp.concatenate([a_r] * rep, axis=1) * c_ref[b, d, h]
                              + jnp.concatenate([s_r] * rep, axis=1) * s1["c_loc"])
            n_ref[b, d, h] = a_r * n_ref[b, d, h] + s_r * s1["n_loc"]
        h_refs[d][0, b] = jnp.concatenate(outs, axis=1).astype(BF16)
        m_ref[b, d] = stats[b, d]["m_new"]


def _mlstm(u_ml, p_big, p_small, ib, fb, tri, trit, ctx_len):
    nb, ta, wq = u_ml.shape
    ncc, nch = ctx_len // CHUNK, ta // CHUNK
    wv = ML_HEADS * ML_V_DIM
    bb = _scan_batch(nb)
    qspecs = [pl.BlockSpec((bb, CHUNK, wq), lambda b, i: (b, _fwd_chunk(i, ncc, nch), 0)),
              pl.BlockSpec((bb, CHUNK, wq), lambda b, i: (b, _bwd_chunk(i, ncc, nch), 0))]
    v_f = pl.BlockSpec((bb, CHUNK, wv), lambda b, i: (b, _fwd_chunk(i, ncc, nch), OFF_MV // wv))
    v_b = pl.BlockSpec((bb, CHUNK, wv), lambda b, i: (b, _bwd_chunk(i, ncc, nch), OFF_MV // wv))
    ps_f = pl.BlockSpec((bb, CHUNK, LANES), lambda b, i: (b, _fwd_chunk(i, ncc, nch), 0))
    ps_b = pl.BlockSpec((bb, CHUNK, LANES), lambda b, i: (b, _bwd_chunk(i, ncc, nch), 0))
    consts = [ib, fb, tri, trit]
    hf, hb = pl.pallas_call(
        _ml_kernel,
        grid=(nb // bb, nch),
        in_specs=qspecs + [v_f, v_b, ps_f, ps_b] + [_full_spec(a) for a in consts],
        out_specs=[pl.BlockSpec((1, bb, CHUNK, wv), lambda b, i: (0, b, _fwd_chunk(i, ncc, nch), 0)),
                   pl.BlockSpec((1, bb, CHUNK, wv), lambda b, i: (0, b, _bwd_chunk(i, ncc, nch), 0))],
        out_shape=[jax.ShapeDtypeStruct((1, nb, ta, wv), BF16)] * 2,
        scratch_shapes=[pltpu.VMEM((bb, 2, ML_HEADS, ML_QK_DIM, ML_V_DIM), F32),
                        pltpu.VMEM((bb, 2, ML_HEADS, ML_QK_DIM, LANES), F32),
                        pltpu.VMEM((bb, 2, 8, CHUNK), F32)],
        compiler_params=_cparams(("arbitrary", "arbitrary")),
        name="mlstm_scan",
    )(u_ml, u_ml, p_big, p_big, p_small, p_small, *consts)
    return hf[0], hb[0]


def _qk_prep_kernel(q_ref, k_ref, cos_ref, sa_ref, sb_ref, gq_ref, gk_ref, gm_ref, qo_ref, ko_ref, *, qscale):
    gm = gm_ref[...]
    nheads = q_ref.shape[2] // LANES
    tm = q_ref.shape[1]
    rb = _row_tile(tm, 192)

    def prep(t, g, mult, cos, sa, sb):
        t = t.astype(F32)
        sq = t * t
        sh = sq.astype(BF16)
        sm = (sq - sh.astype(F32)).astype(BF16)
        ms = (_dot(sh, gm) + _dot(sm, gm)) * (1.0 / DIFF_HEAD_DIM)
        y = t * lax.rsqrt(ms + EPS) * g
        r = y * cos + pltpu.roll(y, LANES - 16, 1) * sa + pltpu.roll(y, 16, 1) * sb
        return (r * mult).astype(BF16)

    for r in range(tm // rb):
        rows = slice(r * rb, (r + 1) * rb)
        tabs = (cos_ref[rows, :], sa_ref[rows, :], sb_ref[rows, :])
        for h in range(nheads):
            sl = slice(h * LANES, (h + 1) * LANES)
            qo_ref[0, rows, sl] = prep(q_ref[0, rows, sl], gq_ref[...], qscale, *tabs)
            ko_ref[0, rows, sl] = prep(k_ref[0, rows, sl], gk_ref[...], 1.0, *tabs)


def _qk_prep(p_big, cos, sa, sb, gq, gk, gm):
    nb, ta, _ = p_big.shape
    w = DIFF_HEADS * 2 * DIFF_HEAD_DIM
    tm = _row_tile(ta, 1056)
    qscale = DIFF_HEAD_DIM ** -0.5 * math.log2(math.e)
    row = lambda b, i: (i, 0)
    return pl.pallas_call(
        functools.partial(_qk_prep_kernel, qscale=qscale),
        grid=(nb, ta // tm),
        in_specs=[pl.BlockSpec((1, tm, w), lambda b, i: (b, i, OFF_Q // w)),
                  pl.BlockSpec((1, tm, w), lambda b, i: (b, i, OFF_K // w)),
                  pl.BlockSpec((tm, LANES), row), pl.BlockSpec((tm, LANES), row), pl.BlockSpec((tm, LANES), row),
                  _full_spec(gq), _full_spec(gk), _full_spec(gm)],
        out_specs=[pl.BlockSpec((1, tm, w), lambda b, i: (b, i, 0))] * 2,
        out_shape=[jax.ShapeDtypeStruct((nb, ta, w), BF16)] * 2,
        compiler_params=_cparams(("arbitrary", "arbitrary")),
        name="qk_prep",
    )(p_big, p_big, cos, sa, sb, gq, gk, gm)


def _attn_kernel(q_ref, k_ref, v_ref, z_ref, lam_ref, gq_ref, gk_ref, sg_ref, *rest, tk, lam_init):
    o_ref, vp_ref, e_ref = rest[-3:]
    qi = pl.program_id(2)
    nkeys = k_ref.shape[1]
    dh = DIFF_HEAD_DIM

    @pl.when(qi == 0)
    def _():
        vp_ref[:, :LANES] = v_ref[0]
        vp_ref[:, LANES:] = jnp.ones((nkeys, LANES), BF16)

    lp = lam_ref[...]
    lam = (jnp.exp(jnp.sum(lp[0:1] * lp[1:2], axis=-1, keepdims=True))
           - jnp.exp(jnp.sum(lp[2:3] * lp[3:4], axis=-1, keepdims=True)) + lam_init)
    shift = (jnp.max(jnp.abs(gq_ref[...]), axis=-1, keepdims=True)
             * jnp.max(jnp.abs(gk_ref[...]), axis=-1, keepdims=True)
             * (dh * dh ** -0.5 * math.log2(math.e)))

    q = q_ref[0]
    lane = lax.broadcasted_iota(jnp.int32, q.shape, 1)
    zero = jnp.zeros_like(q)
    acc = []
    for c, qc in enumerate((jnp.where(lane < dh, q, zero), jnp.where(lane < dh, zero, q))):
        for j in range(nkeys // tk):
            s = _dot_nt(qc, k_ref[0, j * tk:(j + 1) * tk, :])
            e_ref[c, :, j * tk:(j + 1) * tk] = jnp.exp2(s - shift).astype(BF16)
        acc.append(_dot(e_ref[c], vp_ref[...]))
    a0, a1 = acc
    o = a0[:, :LANES] / a0[:, LANES:] - lam * (a1[:, :LANES] / a1[:, LANES:])
    ms = jnp.mean(o * o, axis=-1, keepdims=True)
    o = o * lax.rsqrt(ms + EPS) * sg_ref[...] * (1.0 - lam_init)
    o_ref[0] = (o * _silu(z_ref[0].astype(F32))).astype(BF16)


def _attention(qn, kn, p_big, lam_p, gq, gk, sg, ctx_len, lam_init):
    nb, ta, w = qn.shape
    seq = ta - ctx_len
    tk = 256
    tq = 512 if seq % 512 == 0 else 256
    assert seq % tq == 0 and seq % ctx_len == 0 and ta % tk == 0 and ctx_len % tk == 0
    kern = functools.partial(_attn_kernel, tk=tk, lam_init=lam_init)
    full = lambda a: pl.BlockSpec(a.shape, lambda b, h, i: (0,) * a.ndim)
    consts = [lam_p, gq, gk, sg]

    def call(tq_, nkeys, row_block, key_block, nq, prev):
        in_specs = [pl.BlockSpec((1, tq_, LANES), lambda b, h, i: (b, row_block + i, h)),
                    pl.BlockSpec((1, nkeys, LANES), lambda b, h, i: (b, key_block, h)),
                    pl.BlockSpec((1, nkeys, LANES), lambda b, h, i: (b, key_block, OFF_V // LANES + h)),
                    pl.BlockSpec((1, tq_, LANES), lambda b, h, i: (b, row_block + i, OFF_ZD // LANES + h))]
        in_specs += [full(a) for a in consts]
        args = [qn, kn, p_big, p_big] + consts
        aliases = {}
        if prev is not None:
            in_specs.append(pl.BlockSpec(memory_space=pl.ANY))
            aliases = {len(args): 0}
            args.append(prev)
        return pl.pallas_call(
            kern,
            grid=(nb, DIFF_HEADS, nq),
            in_specs=in_specs,
            out_specs=pl.BlockSpec((1, tq_, LANES), lambda b, h, i: (b, row_block + i, h)),
            out_shape=jax.ShapeDtypeStruct((nb, ta, w), BF16),
            scratch_shapes=[pltpu.VMEM((nkeys, 2 * LANES), BF16), pltpu.VMEM((2, tq_, nkeys), BF16)],
            input_output_aliases=aliases,
            compiler_params=_cparams(("arbitrary", "arbitrary", "arbitrary")),
            name="diff_attention" if prev is None else "diff_attention_ctx",
        )(*args)

    yd = call(tq, ta, 0, 0, seq // tq, None)
    return call(ctx_len, ctx_len, seq // ctx_len, seq // ctx_len, 1, yd)


def _merge_kernel(x_ref, mod_ref, yf_ref, yb_ref, zs_ref, yd_ref, hf_ref, hb_ref, mo_ref, mz_ref, gt_ref,
                  sng_ref, mng_ref, wb_ref, wo_ref, o_ref, *, lat_len, ctx_row):
    b, i = pl.program_id(0), pl.program_id(1)
    tm, d = x_ref.shape[1], x_ref.shape[2]

    ya = (yf_ref[0].astype(F32) + yb_ref[0].astype(F32)) * _silu(zs_ref[0].astype(F32))
    ya = ya * lax.rsqrt(jnp.mean(ya * ya, axis=-1, keepdims=True) + EPS) * sng_ref[...]

    hm = (hf_ref[0].astype(F32) + hb_ref[0].astype(F32)) * _sigmoid(mo_ref[0].astype(F32))
    parts = []
    for h in range(ML_HEADS):
        t = hm[:, h * ML_V_DIM:(h + 1) * ML_V_DIM]
        parts.append(t * lax.rsqrt(jnp.mean(t * t, axis=-1, keepdims=True) + EPS))
    yc = jnp.concatenate(parts, axis=1) * mng_ref[...] * _silu(mz_ref[0].astype(F32))

    g = _sigmoid(gt_ref[0].astype(F32))
    mixed = (g[:, :d] * _dot(ya.astype(BF16), wb_ref[0])
             + g[:, d:2 * d] * _dot(yd_ref[0], wb_ref[1])
             + g[:, 2 * d:] * _dot(yc.astype(BF16), wb_ref[2]))
    out = _dot(mixed.astype(BF16), wo_ref[...])

    row = i * tm + lax.broadcasted_iota(jnp.int32, (tm, 1), 0)
    is_ctx = row >= lat_len
    gate = jnp.where(is_ctx, mod_ref[pl.ds(ctx_row, 1), 2 * d:], mod_ref[pl.ds(b, 1), 2 * d:])
    o_ref[0] = x_ref[0] + gate * out


def _merge(xall, mod_l, yf, yb, p_big, yd, hf, hb, sng, mng, wb, wo, ctx_len):
    nb, ta, d = xall.shape
    tm = _row_tile(ta, 352)
    kern = functools.partial(_merge_kernel, lat_len=ta - ctx_len, ctx_row=nb)
    blk = lambda col: pl.BlockSpec((1, tm, d), lambda b, i: (b, i, col))
    full = lambda a: pl.BlockSpec(a.shape, lambda b, i: (0,) * a.ndim)
    return pl.pallas_call(
        kern,
        grid=(nb, ta // tm),
        in_specs=[blk(0), full(mod_l), blk(0), blk(0), blk(OFF_ZS // d), blk(0), blk(0), blk(0),
                  blk(OFF_MO // d), blk(OFF_MZ // d),
                  pl.BlockSpec((1, tm, 3 * d), lambda b, i: (b, i, OFF_GATES // (3 * d))),
                  full(sng), full(mng), full(wb), full(wo)],
        out_specs=blk(0),
        out_shape=jax.ShapeDtypeStruct((nb, ta, d), F32),
        compiler_params=_cparams(("arbitrary", "arbitrary")),
        name="merge",
    )(xall, mod_l, yf, yb, p_big, yd, hf, hb, p_big, p_big, p_big, sng, mng, wb, wo)


def _rope_tables(ctx_len, seq):
    rows = seq // GRID_W
    row = jnp.repeat(jnp.arange(rows, dtype=F32), GRID_W)
    col = jnp.tile(jnp.arange(GRID_W, dtype=F32), rows)
    half = DIFF_HEAD_DIM // 2
    inv_freq = ROPE_BASE ** (-jnp.arange(0, half, 2, dtype=F32) / half)
    ang_r = row[:, None] * inv_freq
    ang_c = col[:, None] * inv_freq
    cos = jnp.concatenate([jnp.cos(ang_r), jnp.cos(ang_r), jnp.cos(ang_c), jnp.cos(ang_c)], axis=-1)
    sin = jnp.concatenate([jnp.sin(ang_r), jnp.sin(ang_r), jnp.sin(ang_c), jnp.sin(ang_c)], axis=-1)
    cos = jnp.concatenate([cos, jnp.ones((ctx_len, DIFF_HEAD_DIM), F32)], axis=0)
    sin = jnp.concatenate([sin, jnp.zeros((ctx_len, DIFF_HEAD_DIM), F32)], axis=0)
    cos = jnp.tile(cos, (1, LANES // DIFF_HEAD_DIM))
    sin = jnp.tile(sin, (1, LANES // DIFF_HEAD_DIM))
    first = (jnp.arange(LANES) % (DIFF_HEAD_DIM // 2)) < DIFF_HEAD_DIM // 4
    return cos, jnp.where(first, -sin, 0.0), jnp.where(first, 0.0, sin)


def _scan_constants():
    idx = np.arange(CHUNK)
    tri_f = (idx[None, :] <= idx[:, None]).astype(np.float32)
    tri = np.stack([tri_f, tri_f.T])
    trit = np.stack([tri_f.T, tri_f])
    e01 = np.zeros((2, LANES, SSD_HEADS * SSD_HEAD_DIM), np.float32)
    for d in range(2):
        for h in range(SSD_HEADS):
            e01[d, SSD_HEADS * d + h, h * SSD_HEAD_DIM:(h + 1) * SSD_HEAD_DIM] = 1.0
    half = np.arange(LANES) // DIFF_HEAD_DIM
    gm = (half[:, None] == half[None, :]).astype(np.float32)
    return (jnp.asarray(tri, BF16), jnp.asarray(trit, BF16), jnp.asarray(e01, BF16), jnp.asarray(gm, BF16))


def _pad_row(v, offset):
    v = v.reshape(1, -1).astype(F32)
    return jnp.pad(v, ((0, 0), (offset, LANES - offset - v.shape[1])))


def kernel(x, c, ctx, c_ctx, w_mod, b_mod, norm_g, w_in, ssd_conv_w, ssd_conv_b, ssd_a_log, ssd_dt_bias, ssd_d,
           ssd_norm_g, diff_qn_g, diff_kn_g, diff_lambda, diff_subln_g, ml_conv_w, ml_conv_b, ml_i_bias,
           ml_f_bias, ml_norm_g, w_branch, w_out):
    nb, seq, d = x.shape
    ctx_len = ctx.shape[1]
    depth = w_mod.shape[0]
    assert d == 1024 and ctx_len % CHUNK == 0 and seq % CHUNK == 0 and seq % GRID_W == 0 and nb < 8

    sizes = (1536, 1024, 32, 1024, 1024, 1024, 1024, 1024, 1024, 1024, 1024, 8, 8, 3072)
    offs = np.concatenate([[0], np.cumsum(sizes)])
    seg = lambda n: w_in[:, :, offs[n]:offs[n + 1]]
    w_big = jnp.concatenate([seg(1), seg(3), seg(4), seg(5), seg(6), seg(7), seg(8), seg(9), seg(10),
                             seg(13), seg(0)], axis=-1).astype(BF16)
    w_small = jnp.concatenate([seg(2), seg(11), seg(12)], axis=-1)
    w_small = jnp.pad(w_small, ((0, 0), (0, 0), (0, LANES - w_small.shape[-1]))).astype(BF16)

    cc = jnp.concatenate([c, c_ctx[None, :], jnp.zeros((8 - nb - 1, d), F32)], axis=0)
    mod = _modulation(cc, w_mod, b_mod)

    tri, trit, e01, gm = _scan_constants()
    cos, sa, sb = _rope_tables(ctx_len, seq)
    wb = w_branch.astype(BF16)
    wo = w_out.astype(BF16)
    xall = jnp.concatenate([x, ctx], axis=1)

    for l in range(depth):
        lam_init = 0.8 - 0.6 * math.exp(-0.3 * l)
        p_big, p_small = _inproj(xall, mod[l], norm_g[l][None, :], w_big[l], w_small[l], ctx_len)
        u_ssd, u_ml = _conv(p_big, ssd_conv_w[l], ssd_conv_b[l][None, :], ml_conv_w[l], ml_conv_b[l][None, :],
                            ctx_len)
        yf, yb = _ssd(u_ssd, p_small, _pad_row(ssd_dt_bias[l], 0), _pad_row(-jnp.exp(ssd_a_log[l]), 0),
                      jnp.repeat(ssd_d[l], SSD_HEAD_DIM)[None, :], tri, trit, e01, ctx_len)
        qn, kn = _qk_prep(p_big, cos, sa, sb, jnp.tile(diff_qn_g[l], 2)[None, :],
                          jnp.tile(diff_kn_g[l], 2)[None, :], gm)
        yd = _attention(qn, kn, p_big, diff_lambda[l], diff_qn_g[l][None, :], diff_kn_g[l][None, :],
                        diff_subln_g[l][None, :], ctx_len, lam_init)
        hf, hb = _mlstm(u_ml, p_big, p_small, _pad_row(ml_i_bias[l], SM_IG), _pad_row(ml_f_bias[l], SM_FG),
                        tri, trit, ctx_len)
        xall = _merge(xall, mod[l], yf, yb, p_big, yd, hf, hb, ssd_norm_g[l][None, :], ml_norm_g[l][None, :],
                      wb[l], wo[l], ctx_len)
    return xall[:, :seq]
```

```python
import functools
import math

import numpy as np
import jax
import jax.numpy as jnp
from jax import lax
from jax.experimental import pallas as pl
from jax.experimental.pallas import tpu as pltpu

F32 = jnp.float32
BF16 = jnp.bfloat16

GRID_W = 64
EPS = 1e-6
CONV_K = 3
SSD_HEADS = 16
SSD_HEAD_DIM = 64
SSD_GROUPS = 2
SSD_STATE = 128
DIFF_HEADS = 8
DIFF_HEAD_DIM = 64
ROPE_BASE = 10000.0
ML_HEADS = 4
ML_QK_DIM = 128
ML_V_DIM = 256
CHUNK = 128
LANES = 128
BF16_ROWS = 16

OFF_ZS, OFF_Q, OFF_K, OFF_V, OFF_ZD = 0, 1024, 2048, 3072, 4096
OFF_MQK, OFF_MV, OFF_MO, OFF_MZ, OFF_GATES, OFF_XBC = 5120, 6144, 7168, 8192, 9216, 12288
N_BIG = 13824
SM_IG, SM_FG = 32, 40

VMEM_LIMIT = 56 * 1024 * 1024


def _cparams(sem):
    return pltpu.CompilerParams(dimension_semantics=sem, vmem_limit_bytes=VMEM_LIMIT)


def _split3(v):
    h = v.astype(BF16)
    r = v - h.astype(F32)
    m = r.astype(BF16)
    l = (r - m.astype(F32)).astype(BF16)
    return h, m, l


def _dot(a, b):
    return jnp.dot(a, b, preferred_element_type=F32)


def _dot_nt(a, b):
    return lax.dot_general(a, b, (((1,), (1,)), ((), ())), preferred_element_type=F32)


def _dot_tn(a, b):
    return lax.dot_general(a, b, (((0,), (0,)), ((), ())), preferred_element_type=F32)


def _dot_exact_rhs01(v, m01x3):
    return _dot(jnp.concatenate(_split3(v), axis=1), m01x3)


def _expand_rhs01(v, m01x2):
    h = v.astype(BF16)
    m = (v - h.astype(F32)).astype(BF16)
    return _dot(jnp.concatenate([h, m], axis=1), m01x2)


def _dot_exact_lhs01(m01x3, v):
    return _dot(m01x3, jnp.concatenate(_split3(v), axis=0))


def _sigmoid(x):
    return 1.0 / (1.0 + jnp.exp2(x * (-math.log2(math.e))))


def _silu(x):
    return x * _sigmoid(x)


def _softplus(x):
    return jnp.maximum(x, 0.0) + jnp.log(1.0 + jnp.exp(-jnp.abs(x)))


def _mod_kernel(cc_ref, w_ref, b_ref, o_ref):
    a = _silu(cc_ref[...])
    w = w_ref[0]
    ah, am, _ = _split3(a)
    wh = w.astype(BF16)
    wm = (w - wh.astype(F32)).astype(BF16)
    o_ref[0] = _dot(ah, wh) + _dot(am, wh) + _dot(ah, wm) + b_ref[0]


def _modulation(cc, w_mod, b_mod):
    depth, d, d3 = w_mod.shape
    tn = 1024
    return pl.pallas_call(
        _mod_kernel,
        grid=(depth, d3 // tn),
        in_specs=[pl.BlockSpec((8, d), lambda l, j: (0, 0)),
                  pl.BlockSpec((1, d, tn), lambda l, j: (l, 0, j)),
                  pl.BlockSpec((1, 1, tn), lambda l, j: (l, 0, j))],
        out_specs=pl.BlockSpec((1, 8, tn), lambda l, j: (l, 0, j)),
        out_shape=jax.ShapeDtypeStruct((depth, 8, d3), F32),
        compiler_params=_cparams(("arbitrary", "arbitrary")),
        name="modulation",
    )(cc, w_mod, b_mod.reshape(depth, 1, d3))


def _inproj_kernel(x_ref, mod_ref, g_ref, w_ref, ws_ref, o_ref, os_ref, h_ref, *, lat_len, ctx_row):
    b, i, j = pl.program_id(0), pl.program_id(1), pl.program_id(2)
    tm, d = h_ref.shape

    @pl.when(j == 0)
    def _():
        x = x_ref[0]
        ms = jnp.mean(x * x, axis=-1, keepdims=True)
        y = x * lax.rsqrt(ms + EPS) * g_ref[...]
        row = i * tm + lax.broadcasted_iota(jnp.int32, (tm, 1), 0)
        is_ctx = row >= lat_len
        mlat = mod_ref[pl.ds(b, 1), :]
        mctx = mod_ref[pl.ds(ctx_row, 1), :]
        shift = jnp.where(is_ctx, mctx[:, :d], mlat[:, :d])
        scale = jnp.where(is_ctx, mctx[:, d:2 * d], mlat[:, d:2 * d])
        h = (y * (1.0 + scale) + shift).astype(BF16)
        h_ref[...] = h
        os_ref[0] = _dot(h, ws_ref[...])

    o_ref[0] = _dot(h_ref[...], w_ref[...]).astype(BF16)


def _row_tile(ta, target):
    best = BF16_ROWS
    for t in range(BF16_ROWS, target + 1, BF16_ROWS):
        if ta % t == 0:
            best = t
    return best


def _inproj(xall, mod_l, g, w_big, w_small, ctx_len):
    nb, ta, d = xall.shape
    tm = _row_tile(ta, 1056)
    tn = 1536
    kern = functools.partial(_inproj_kernel, lat_len=ta - ctx_len, ctx_row=nb)
    return pl.pallas_call(
        kern,
        grid=(nb, ta // tm, N_BIG // tn),
        in_specs=[pl.BlockSpec((1, tm, d), lambda b, i, j: (b, i, 0)),
                  pl.BlockSpec(mod_l.shape, lambda b, i, j: (0, 0)),
                  pl.BlockSpec((1, d), lambda b, i, j: (0, 0)),
                  pl.BlockSpec((d, tn), lambda b, i, j: (0, j)),
                  pl.BlockSpec((d, LANES), lambda b, i, j: (0, 0))],
        out_specs=[pl.BlockSpec((1, tm, tn), lambda b, i, j: (b, i, j)),
                   pl.BlockSpec((1, tm, LANES), lambda b, i, j: (b, i, 0))],
        out_shape=[jax.ShapeDtypeStruct((nb, ta, N_BIG), BF16),
                   jax.ShapeDtypeStruct((nb, ta, LANES), F32)],
        scratch_shapes=[pltpu.VMEM((tm, d), BF16)],
        compiler_params=_cparams(("arbitrary", "arbitrary", "arbitrary")),
        name="inproj",
    )(xall, mod_l, g, w_big, w_small)


def _scan_batch(nb):
    return 4 if nb % 4 == 0 else (2 if nb % 2 == 0 else 1)


def _fwd_chunk(i, ncc, nch):
    return jnp.where(i < ncc, nch - ncc + i, i - ncc)


def _bwd_chunk(i, ncc, nch):
    del ncc
    return nch - 1 - i


def _full_spec(a):
    nd = a.ndim
    return pl.BlockSpec(a.shape, lambda b, i: (0,) * nd)


def _conv_kernel(xs_ref, xsp_ref, xsn_ref, xm_ref, xmp_ref, xmn_ref, cws_ref, cbs_ref, cwm_ref, cbm_ref,
                 us_ref, um_ref, *, lat_len, ta):
    tm = xs_ref.shape[1]
    row0 = pl.program_id(1) * tm
    has_prev = jnp.logical_and(row0 != 0, row0 != lat_len)
    has_next = jnp.logical_and(row0 + tm != lat_len, row0 + tm != ta)
    rid = lax.broadcasted_iota(jnp.int32, (tm, LANES), 0)
    for x_ref, xp_ref, xn_ref, cw_ref, cb_ref, o_ref in ((xs_ref, xsp_ref, xsn_ref, cws_ref, cbs_ref, us_ref),
                                                         (xm_ref, xmp_ref, xmn_ref, cwm_ref, cbm_ref, um_ref)):
        for j in range(x_ref.shape[2] // LANES):
            sl = slice(j * LANES, (j + 1) * LANES)
            x = x_ref[0, :, sl].astype(F32)
            prow = jnp.where(has_prev, xp_ref[0, BF16_ROWS - 1:BF16_ROWS, sl].astype(F32), 0.0)
            nrow = jnp.where(has_next, xn_ref[0, 0:1, sl].astype(F32), 0.0)
            xprev = jnp.where(rid == 0, prow, pltpu.roll(x, 1, 0))
            xnext = jnp.where(rid == tm - 1, nrow, pltpu.roll(x, tm - 1, 0))
            u = xprev * cw_ref[0:1, sl] + x * cw_ref[1:2, sl] + xnext * cw_ref[2:3, sl] + cb_ref[:, sl]
            o_ref[0, :, sl] = _silu(u).astype(BF16)


def _conv(p_big, cws, cbs, cwm, cbm, ctx_len):
    nb, ta, _ = p_big.shape
    tm = _row_tile(math.gcd(ta - ctx_len, ctx_len), 256)
    per = tm // BF16_ROWS
    nrow16 = ta // BF16_ROWS
    ws, wm = cws.shape[1], cwm.shape[1]

    def specs(width, col_block):
        return [pl.BlockSpec((1, tm, width), lambda b, i: (b, i, col_block)),
                pl.BlockSpec((1, BF16_ROWS, width), lambda b, i: (b, jnp.maximum(i * per - 1, 0), col_block)),
                pl.BlockSpec((1, BF16_ROWS, width),
                             lambda b, i: (b, jnp.minimum((i + 1) * per, nrow16 - 1), col_block))]

    consts = [cws, cbs, cwm, cbm]
    return pl.pallas_call(
        functools.partial(_conv_kernel, lat_len=ta - ctx_len, ta=ta),
        grid=(nb, ta // tm),
        in_specs=specs(ws, OFF_XBC // ws) + specs(wm, OFF_MQK // wm) + [_full_spec(a) for a in consts],
        out_specs=[pl.BlockSpec((1, tm, ws), lambda b, i: (b, i, 0)),
                   pl.BlockSpec((1, tm, wm), lambda b, i: (b, i, 0))],
        out_shape=[jax.ShapeDtypeStruct((nb, ta, ws), BF16), jax.ShapeDtypeStruct((nb, ta, wm), BF16)],
        compiler_params=_cparams(("arbitrary", "arbitrary")),
        name="conv_silu",
    )(p_big, p_big, p_big, p_big, p_big, p_big, *consts)


def _ssd_stats(d, ps, dtb_ref, a_ref, tri3_ref, trit3_ref, e_ref):
    L = CHUNK
    dt = _softplus(ps + dtb_ref[...])
    a = dt * a_ref[...]
    acs = _dot_exact_lhs01(tri3_ref[d], a)
    acs_t = _dot_exact_rhs01(a.T, trit3_ref[d])
    last = L - 1 if d == 0 else 0
    dend = jnp.exp(acs[last:last + 1, :] - acs)
    e01 = e_ref[d]
    eacs_e = _expand_rhs01(jnp.exp(acs), e01)
    return dict(acs=acs, acs_t=acs_t, dt_e=_expand_rhs01(dt, e01), dtend_e=_expand_rhs01(dt * dend, e01),
                eacs_e=eacs_e, cd_e=eacs_e[last:last + 1, :])


def _ssd_kernel(uf, ub, psf, psb, dtb, aneg, dsk, tri, tri3, trit3, e01, yf, yb, s_ref):
    @pl.when(pl.program_id(1) == 0)
    def _():
        s_ref[...] = jnp.zeros_like(s_ref)

    L = CHUNK
    hp = SSD_HEADS * SSD_HEAD_DIM
    gn = SSD_GROUPS * SSD_STATE
    gw = hp // SSD_GROUPS
    hpg = SSD_HEADS // SSD_GROUPS
    u_refs, y_refs, ps_refs = (uf, ub), (yf, yb), (psf, psb)
    bds = [(b, d) for b in range(uf.shape[0]) for d in range(2)]
    dgs = [(b, d, g) for b, d in bds for g in range(SSD_GROUPS)]

    cg, bg, sg, yo = {}, {}, {}, {}
    for b, d, g in dgs:
        bg[b, d, g] = u_refs[d][b, :, hp + g * SSD_STATE:hp + (g + 1) * SSD_STATE]
        cg[b, d, g] = u_refs[d][b, :, hp + gn + g * SSD_STATE:hp + gn + (g + 1) * SSD_STATE]
        sg[b, d, g] = _dot_nt(cg[b, d, g], bg[b, d, g])
        yo[b, d, g] = _dot(cg[b, d, g], s_ref[b, d, g].astype(BF16))
    st = {(b, d): _ssd_stats(d, ps_refs[d][b], dtb, aneg, tri3, trit3, e01) for b, d in bds}
    xs = {(b, d): u_refs[d][b, :, :hp].astype(F32) for b, d in bds}
    masks = [tri[d] > 0 for d in range(2)]
    left = lax.broadcasted_iota(jnp.int32, (L, LANES), 1) < SSD_HEAD_DIM

    pairs, snew = {}, {}
    for b, d, g in dgs:
        sd = st[b, d]
        xd = xs[b, d][:, g * gw:(g + 1) * gw] * sd["dt_e"][:, g * gw:(g + 1) * gw]
        for k in range(hpg // 2):
            ms = []
            for h in (g * hpg + 2 * k, g * hpg + 2 * k + 1):
                j = SSD_HEADS * d + h
                seg = sd["acs"][:, j:j + 1] - sd["acs_t"][j:j + 1, :]
                ms.append((sg[b, d, g] * jnp.exp(jnp.where(masks[d], seg, -jnp.inf))).astype(BF16))
            xp = xd[:, 2 * k * SSD_HEAD_DIM:(2 * k + 2) * SSD_HEAD_DIM]
            rhs = jnp.concatenate([jnp.where(left, xp, 0.0), jnp.where(left, 0.0, xp)], axis=0).astype(BF16)
            pairs[b, d, g, k] = _dot(jnp.concatenate(ms, axis=1), rhs)
        xde = (xs[b, d][:, g * gw:(g + 1) * gw] * sd["dtend_e"][:, g * gw:(g + 1) * gw]).astype(BF16)
        snew[b, d, g] = _dot_tn(bg[b, d, g], xde)

    for b, d in bds:
        ys = []
        for g in range(SSD_GROUPS):
            sl = slice(g * gw, (g + 1) * gw)
            ys.append(jnp.concatenate([pairs[b, d, g, k] for k in range(hpg // 2)], axis=1)
                      + yo[b, d, g] * st[b, d]["eacs_e"][:, sl])
            s_ref[b, d, g] = s_ref[b, d, g] * st[b, d]["cd_e"][:, sl] + snew[b, d, g]
        y = jnp.concatenate(ys, axis=1)
        if d == 0:
            y = y + dsk[...] * xs[b, d]
        y_refs[d][0, b] = y.astype(BF16)


def _ssd(u_ssd, p_small, dtb, aneg, dsk, tri, tri3, trit3, e01, ctx_len):
    nb, ta, width = u_ssd.shape
    ncc, nch = ctx_len // CHUNK, ta // CHUNK
    hp = SSD_HEADS * SSD_HEAD_DIM
    bb = _scan_batch(nb)
    xspecs = [pl.BlockSpec((bb, CHUNK, width), lambda b, i: (b, _fwd_chunk(i, ncc, nch), 0)),
              pl.BlockSpec((bb, CHUNK, width), lambda b, i: (b, _bwd_chunk(i, ncc, nch), 0))]
    ps_f = pl.BlockSpec((bb, CHUNK, LANES), lambda b, i: (b, _fwd_chunk(i, ncc, nch), 0))
    ps_b = pl.BlockSpec((bb, CHUNK, LANES), lambda b, i: (b, _bwd_chunk(i, ncc, nch), 0))
    consts = [dtb, aneg, dsk, tri, tri3, trit3, e01]
    yf, yb = pl.pallas_call(
        _ssd_kernel,
        grid=(nb // bb, nch),
        in_specs=xspecs + [ps_f, ps_b] + [_full_spec(a) for a in consts],
        out_specs=[pl.BlockSpec((1, bb, CHUNK, hp), lambda b, i: (0, b, _fwd_chunk(i, ncc, nch), 0)),
                   pl.BlockSpec((1, bb, CHUNK, hp), lambda b, i: (0, b, _bwd_chunk(i, ncc, nch), 0))],
        out_shape=[jax.ShapeDtypeStruct((1, nb, ta, hp), BF16)] * 2,
        scratch_shapes=[pltpu.VMEM((bb, 2, SSD_GROUPS, SSD_STATE, hp // SSD_GROUPS), F32)],
        compiler_params=_cparams(("arbitrary", "arbitrary")),
        name="ssd_scan",
    )(u_ssd, u_ssd, p_small, p_small, *consts)
    return yf[0], yb[0]


def _ml_stats(d, ps, ib_ref, fb_ref, trit3_ref, m_prev):
    L = CHUNK
    li_t = (ps + ib_ref[...]).T
    lf_t = (-_softplus(-(ps + fb_ref[...]))).T
    bc = _dot_exact_rhs01(lf_t, trit3_ref[d])[SM_FG:SM_FG + 8, :]
    lir = li_t[SM_IG:SM_IG + 8, :]
    last = L - 1 if d == 0 else 0
    lane = lax.broadcasted_iota(jnp.int32, (8, L), 1)
    b_last = jnp.broadcast_to(bc[:, last:last + 1], (8, L))

    gl = b_last - bc + lir
    m_loc = jnp.broadcast_to(jnp.max(gl, axis=-1, keepdims=True), (8, L))
    m_new = jnp.maximum(b_last + m_prev, m_loc)

    g = lir - bc
    sh = 1
    while sh < L:
        if d == 0:
            g = jnp.maximum(g, jnp.where(lane >= sh, pltpu.roll(g, sh, 1), -jnp.inf))
        else:
            g = jnp.maximum(g, jnp.where(lane < L - sh, pltpu.roll(g, L - sh, 1), -jnp.inf))
        sh *= 2
    e = bc + m_prev
    m_t = jnp.maximum(e, bc + g)
    cols = jnp.concatenate([bc - m_t, jnp.exp(e - m_t), jnp.exp(-m_t), jnp.zeros((L - 24, L), F32)], axis=0).T
    return dict(w=jnp.exp(gl - m_loc), a_s=jnp.exp(b_last + m_prev - m_new), s_s=jnp.exp(m_loc - m_new),
                m_new=m_new, rowb=bc - lir, cols=cols)


def _ml_kernel(uf, ub, vf, vb, psf, psb, ib, fb, tri, trit3, hf, hb, c_ref, n_ref, m_ref):
    @pl.when(pl.program_id(1) == 0)
    def _():
        c_ref[...] = jnp.zeros_like(c_ref)
        n_ref[...] = jnp.zeros_like(n_ref)
        m_ref[...] = jnp.zeros_like(m_ref)

    L = CHUNK
    hq = ML_HEADS * ML_QK_DIM
    rep = ML_V_DIM // LANES
    ones = jnp.ones((L, LANES), BF16)
    u_refs, v_refs, h_refs, ps_refs = (uf, ub), (vf, vb), (hf, hb), (psf, psb)
    masks = [tri[d] > 0 for d in range(2)]
    bds = [(b, d) for b in range(uf.shape[0]) for d in range(2)]
    units = [(b, d, h) for b, d in bds for h in range(ML_HEADS)]

    st1 = {}
    for b, d, h in units:
        q = u_refs[d][b, :, h * ML_QK_DIM:(h + 1) * ML_QK_DIM]
        k = u_refs[d][b, :, hq + h * ML_QK_DIM:hq + (h + 1) * ML_QK_DIM].astype(F32) * (ML_QK_DIM ** -0.5)
        st1[b, d, h] = dict(k=k, s=_dot_nt(q, k.astype(BF16)), qc=_dot(q, c_ref[b, d, h].astype(BF16)),
                            qn=_dot(q, n_ref[b, d, h].astype(BF16)))
    stats = {(b, d): _ml_stats(d, ps_refs[d][b], ib, fb, trit3, m_ref[b, d]) for b, d in bds}
    for b, d, h in units:
        r = ML_HEADS * d + h
        v = v_refs[d][b, :, h * ML_V_DIM:(h + 1) * ML_V_DIM]
        kwt = (st1[b, d, h]["k"].T * stats[b, d]["w"][r:r + 1, :]).astype(BF16)
        st1[b, d, h].update(v=v, c_loc=_dot(kwt, v), n_loc=_dot(kwt, ones))
    st2 = {}
    for b, d, h in units:
        r = ML_HEADS * d + h
        cola = stats[b, d]["cols"][:, r:r + 1]
        x = jnp.exp(jnp.where(masks[d], cola - stats[b, d]["rowb"][r:r + 1, :], -jnp.inf))
        wts = (x * st1[b, d, h]["s"]).astype(BF16)
        st2[b, d, h] = (_dot(wts, st1[b, d, h]["v"]), _dot(wts, ones))
    for b, d in bds:
        outs = []
        for h in range(ML_HEADS):
            r = ML_HEADS * d + h
            s1 = st1[b, d, h]
            sc = stats[b, d]["cols"][:, 8 + r:9 + r]
            emt = stats[b, d]["cols"][:, 16 + r:17 + r]
            wv, wo = st2[b, d, h]
            num = wv + s1["qc"] * sc
            den = wo + s1["qn"] * sc
            rden = 1.0 / jnp.maximum(jnp.abs(den), emt)
            outs.append(num * jnp.concatenate([rden] * rep, axis=1))
            a_r = stats[b, d]["a_s"][r:r + 1, :]
            s_r = stats[b, d]["s_s"][r:r + 1, :]
            c_ref[b, d, h] = (jnp.concatenate([a_r] * rep, axis=1) * c_ref[b, d, h]
                              + jnp.concatenate([s_r] * rep, axis=1) * s1["c_loc"])
            n_ref[b, d, h] = a_r * n_ref[b, d, h] + s_r * s1["n_loc"]
        h_refs[d][0, b] = jnp.concatenate(outs, axis=1).astype(BF16)
        m_ref[b, d] = stats[b, d]["m_new"]


def _mlstm(u_ml, p_big, p_small, ib, fb, tri, trit3, ctx_len):
    nb, ta, wq = u_ml.shape
    ncc, nch = ctx_len // CHUNK, ta // CHUNK
    wv = ML_HEADS * ML_V_DIM
    bb = _scan_batch(nb)
    qspecs = [pl.BlockSpec((bb, CHUNK, wq), lambda b, i: (b, _fwd_chunk(i, ncc, nch), 0)),
              pl.BlockSpec((bb, CHUNK, wq), lambda b, i: (b, _bwd_chunk(i, ncc, nch), 0))]
    v_f = pl.BlockSpec((bb, CHUNK, wv), lambda b, i: (b, _fwd_chunk(i, ncc, nch), OFF_MV // wv))
    v_b = pl.BlockSpec((bb, CHUNK, wv), lambda b, i: (b, _bwd_chunk(i, ncc, nch), OFF_MV // wv))
    ps_f = pl.BlockSpec((bb, CHUNK, LANES), lambda b, i: (b, _fwd_chunk(i, ncc, nch), 0))
    ps_b = pl.BlockSpec((bb, CHUNK, LANES), lambda b, i: (b, _bwd_chunk(i, ncc, nch), 0))
    consts = [ib, fb, tri, trit3]
    hf, hb = pl.pallas_call(
        _ml_kernel,
        grid=(nb // bb, nch),
        in_specs=qspecs + [v_f, v_b, ps_f, ps_b] + [_full_spec(a) for a in consts],
        out_specs=[pl.BlockSpec((1, bb, CHUNK, wv), lambda b, i: (0, b, _fwd_chunk(i, ncc, nch), 0)),
                   pl.BlockSpec((1, bb, CHUNK, wv), lambda b, i: (0, b, _bwd_chunk(i, ncc, nch), 0))],
        out_shape=[jax.ShapeDtypeStruct((1, nb, ta, wv), BF16)] * 2,
        scratch_shapes=[pltpu.VMEM((bb, 2, ML_HEADS, ML_QK_DIM, ML_V_DIM), F32),
                        pltpu.VMEM((bb, 2, ML_HEADS, ML_QK_DIM, LANES), F32),
                        pltpu.VMEM((bb, 2, 8, CHUNK), F32)],
        compiler_params=_cparams(("arbitrary", "arbitrary")),
        name="mlstm_scan",
    )(u_ml, u_ml, p_big, p_big, p_small, p_small, *consts)
    return hf[0], hb[0]


def _qk_prep_kernel(q_ref, k_ref, cos_ref, sa_ref, sb_ref, gq_ref, gk_ref, gm_ref, qo_ref, ko_ref, *, qscale):
    gm = gm_ref[...]
    nheads = q_ref.shape[2] // LANES
    tm = q_ref.shape[1]
    rb = _row_tile(tm, 192)

    def prep(t, g, mult, cos, sa, sb):
        t = t.astype(F32)
        sq = t * t
        sh = sq.astype(BF16)
        sm = (sq - sh.astype(F32)).astype(BF16)
        ms = (_dot(sh, gm) + _dot(sm, gm)) * (1.0 / DIFF_HEAD_DIM)
        y = t * lax.rsqrt(ms + EPS) * g
        r = y * cos + pltpu.roll(y, LANES - 16, 1) * sa + pltpu.roll(y, 16, 1) * sb
        return (r * mult).astype(BF16)

    for r in range(tm // rb):
        rows = slice(r * rb, (r + 1) * rb)
        tabs = (cos_ref[rows, :], sa_ref[rows, :], sb_ref[rows, :])
        for h in range(nheads):
            sl = slice(h * LANES, (h + 1) * LANES)
            qo_ref[0, rows, sl] = prep(q_ref[0, rows, sl], gq_ref[...], qscale, *tabs)
            ko_ref[0, rows, sl] = prep(k_ref[0, rows, sl], gk_ref[...], 1.0, *tabs)


def _qk_prep(p_big, cos, sa, sb, gq, gk, gm):
    nb, ta, _ = p_big.shape
    w = DIFF_HEADS * 2 * DIFF_HEAD_DIM
    tm = _row_tile(ta, 1056)
    qscale = DIFF_HEAD_DIM ** -0.5 * math.log2(math.e)
    row = lambda b, i: (i, 0)
    return pl.pallas_call(
        functools.partial(_qk_prep_kernel, qscale=qscale),
        grid=(nb, ta // tm),
        in_specs=[pl.BlockSpec((1, tm, w), lambda b, i: (b, i, OFF_Q // w)),
                  pl.BlockSpec((1, tm, w), lambda b, i: (b, i, OFF_K // w)),
                  pl.BlockSpec((tm, LANES), row), pl.BlockSpec((tm, LANES), row), pl.BlockSpec((tm, LANES), row),
                  _full_spec(gq), _full_spec(gk), _full_spec(gm)],
        out_specs=[pl.BlockSpec((1, tm, w), lambda b, i: (b, i, 0))] * 2,
        out_shape=[jax.ShapeDtypeStruct((nb, ta, w), BF16)] * 2,
        compiler_params=_cparams(("arbitrary", "arbitrary")),
        name="qk_prep",
    )(p_big, p_big, cos, sa, sb, gq, gk, gm)


def _attn_kernel(q_ref, k_ref, v_ref, z_ref, lam_ref, gq_ref, gk_ref, sg_ref, *rest, tk, lam_init):
    o_ref, vp_ref, e_ref = rest[-3:]
    qi = pl.program_id(2)
    nkeys = k_ref.shape[1]
    dh = DIFF_HEAD_DIM

    @pl.when(qi == 0)
    def _():
        vp_ref[:, :LANES] = v_ref[0]
        vp_ref[:, LANES:] = jnp.ones((nkeys, LANES), BF16)

    lp = lam_ref[...]
    lam = (jnp.exp(jnp.sum(lp[0:1] * lp[1:2], axis=-1, keepdims=True))
           - jnp.exp(jnp.sum(lp[2:3] * lp[3:4], axis=-1, keepdims=True)) + lam_init)
    shift = (jnp.max(jnp.abs(gq_ref[...]), axis=-1, keepdims=True)
             * jnp.max(jnp.abs(gk_ref[...]), axis=-1, keepdims=True)
             * (dh * dh ** -0.5 * math.log2(math.e)))

    q = q_ref[0]
    lane = lax.broadcasted_iota(jnp.int32, q.shape, 1)
    zero = jnp.zeros_like(q)
    acc = []
    for c, qc in enumerate((jnp.where(lane < dh, q, zero), jnp.where(lane < dh, zero, q))):
        for j in range(nkeys // tk):
            s = _dot_nt(qc, k_ref[0, j * tk:(j + 1) * tk, :])
            e_ref[c, :, j * tk:(j + 1) * tk] = jnp.exp2(s - shift).astype(BF16)
        acc.append(_dot(e_ref[c], vp_ref[...]))
    a0, a1 = acc
    o = a0[:, :LANES] / a0[:, LANES:] - lam * (a1[:, :LANES] / a1[:, LANES:])
    ms = jnp.mean(o * o, axis=-1, keepdims=True)
    o = o * lax.rsqrt(ms + EPS) * sg_ref[...] * (1.0 - lam_init)
    o_ref[0] = (o * _silu(z_ref[0].astype(F32))).astype(BF16)


def _attention(qn, kn, p_big, lam_p, gq, gk, sg, ctx_len, lam_init):
    nb, ta, w = qn.shape
    seq = ta - ctx_len
    tk = 256
    tq = 1024 if seq % 1024 == 0 else 256
    assert seq % tq == 0 and seq % ctx_len == 0 and ta % tk == 0 and ctx_len % tk == 0
    kern = functools.partial(_attn_kernel, tk=tk, lam_init=lam_init)
    full = lambda a: pl.BlockSpec(a.shape, lambda b, h, i: (0,) * a.ndim)
    consts = [lam_p, gq, gk, sg]

    def call(tq_, nkeys, row_block, key_block, nq, prev):
        in_specs = [pl.BlockSpec((1, tq_, LANES), lambda b, h, i: (b, row_block + i, h)),
                    pl.BlockSpec((1, nkeys, LANES), lambda b, h, i: (b, key_block, h)),
                    pl.BlockSpec((1, nkeys, LANES), lambda b, h, i: (b, key_block, OFF_V // LANES + h)),
                    pl.BlockSpec((1, tq_, LANES), lambda b, h, i: (b, row_block + i, OFF_ZD // LANES + h))]
        in_specs += [full(a) for a in consts]
        args = [qn, kn, p_big, p_big] + consts
        aliases = {}
        if prev is not None:
            in_specs.append(pl.BlockSpec(memory_space=pl.ANY))
            aliases = {len(args): 0}
            args.append(prev)
        return pl.pallas_call(
            kern,
            grid=(nb, DIFF_HEADS, nq),
            in_specs=in_specs,
            out_specs=pl.BlockSpec((1, tq_, LANES), lambda b, h, i: (b, row_block + i, h)),
            out_shape=jax.ShapeDtypeStruct((nb, ta, w), BF16),
            scratch_shapes=[pltpu.VMEM((nkeys, 2 * LANES), BF16), pltpu.VMEM((2, tq_, nkeys), BF16)],
            input_output_aliases=aliases,
            compiler_params=_cparams(("arbitrary", "arbitrary", "arbitrary")),
            name="diff_attention" if prev is None else "diff_attention_ctx",
        )(*args)

    yd = call(tq, ta, 0, 0, seq // tq, None)
    return call(ctx_len, ctx_len, seq // ctx_len, seq // ctx_len, 1, yd)


def _merge_kernel(x_ref, mod_ref, yf_ref, yb_ref, zs_ref, yd_ref, hf_ref, hb_ref, mo_ref, mz_ref, gt_ref,
                  sng_ref, mng_ref, wb_ref, wo_ref, o_ref, *, lat_len, ctx_row):
    b, i = pl.program_id(0), pl.program_id(1)
    tm, d = x_ref.shape[1], x_ref.shape[2]

    pd = _dot(yd_ref[0], wb_ref[1])
    ya = (yf_ref[0].astype(F32) + yb_ref[0].astype(F32)) * _silu(zs_ref[0].astype(F32))
    ya = ya * lax.rsqrt(jnp.mean(ya * ya, axis=-1, keepdims=True) + EPS) * sng_ref[...]
    pa = _dot(ya.astype(BF16), wb_ref[0])

    hm = (hf_ref[0].astype(F32) + hb_ref[0].astype(F32)) * _sigmoid(mo_ref[0].astype(F32))
    parts = []
    for h in range(ML_HEADS):
        t = hm[:, h * ML_V_DIM:(h + 1) * ML_V_DIM]
        parts.append(t * lax.rsqrt(jnp.mean(t * t, axis=-1, keepdims=True) + EPS))
    yc = jnp.concatenate(parts, axis=1) * mng_ref[...] * _silu(mz_ref[0].astype(F32))

    g = _sigmoid(gt_ref[0].astype(F32))
    mixed = (g[:, :d] * pa + g[:, d:2 * d] * pd + g[:, 2 * d:] * _dot(yc.astype(BF16), wb_ref[2]))
    out = _dot(mixed.astype(BF16), wo_ref[...])

    row = i * tm + lax.broadcasted_iota(jnp.int32, (tm, 1), 0)
    is_ctx = row >= lat_len
    gate = jnp.where(is_ctx, mod_ref[pl.ds(ctx_row, 1), 2 * d:], mod_ref[pl.ds(b, 1), 2 * d:])
    o_ref[0] = x_ref[0] + gate * out


def _merge(xall, mod_l, yf, yb, p_big, yd, hf, hb, sng, mng, wb, wo, ctx_len):
    nb, ta, d = xall.shape
    tm = _row_tile(ta, 352)
    kern = functools.partial(_merge_kernel, lat_len=ta - ctx_len, ctx_row=nb)
    blk = lambda col: pl.BlockSpec((1, tm, d), lambda b, i: (b, i, col))
    full = lambda a: pl.BlockSpec(a.shape, lambda b, i: (0,) * a.ndim)
    return pl.pallas_call(
        kern,
        grid=(nb, ta // tm),
        in_specs=[blk(0), full(mod_l), blk(0), blk(0), blk(OFF_ZS // d), blk(0), blk(0), blk(0),
                  blk(OFF_MO // d), blk(OFF_MZ // d),
                  pl.BlockSpec((1, tm, 3 * d), lambda b, i: (b, i, OFF_GATES // (3 * d))),
                  full(sng), full(mng), full(wb), full(wo)],
        out_specs=blk(0),
        out_shape=jax.ShapeDtypeStruct((nb, ta, d), F32),
        compiler_params=_cparams(("arbitrary", "arbitrary")),
        name="merge",
    )(xall, mod_l, yf, yb, p_big, yd, hf, hb, p_big, p_big, p_big, sng, mng, wb, wo)


def _rope_tables(ctx_len, seq):
    rows = seq // GRID_W
    row = jnp.repeat(jnp.arange(rows, dtype=F32), GRID_W)
    col = jnp.tile(jnp.arange(GRID_W, dtype=F32), rows)
    half = DIFF_HEAD_DIM // 2
    inv_freq = ROPE_BASE ** (-jnp.arange(0, half, 2, dtype=F32) / half)
    ang_r = row[:, None] * inv_freq
    ang_c = col[:, None] * inv_freq
    cos = jnp.concatenate([jnp.cos(ang_r), jnp.cos(ang_r), jnp.cos(ang_c), jnp.cos(ang_c)], axis=-1)
    sin = jnp.concatenate([jnp.sin(ang_r), jnp.sin(ang_r), jnp.sin(ang_c), jnp.sin(ang_c)], axis=-1)
    cos = jnp.concatenate([cos, jnp.ones((ctx_len, DIFF_HEAD_DIM), F32)], axis=0)
    sin = jnp.concatenate([sin, jnp.zeros((ctx_len, DIFF_HEAD_DIM), F32)], axis=0)
    cos = jnp.tile(cos, (1, LANES // DIFF_HEAD_DIM))
    sin = jnp.tile(sin, (1, LANES // DIFF_HEAD_DIM))
    first = (jnp.arange(LANES) % (DIFF_HEAD_DIM // 2)) < DIFF_HEAD_DIM // 4
    return cos, jnp.where(first, -sin, 0.0), jnp.where(first, 0.0, sin)


def _scan_constants():
    idx = np.arange(CHUNK)
    tri_f = (idx[None, :] <= idx[:, None]).astype(np.float32)
    tri = np.stack([tri_f, tri_f.T])
    trit = np.stack([tri_f.T, tri_f])
    e01 = np.zeros((2, LANES, SSD_HEADS * SSD_HEAD_DIM), np.float32)
    for d in range(2):
        for h in range(SSD_HEADS):
            e01[d, SSD_HEADS * d + h, h * SSD_HEAD_DIM:(h + 1) * SSD_HEAD_DIM] = 1.0
    half = np.arange(LANES) // DIFF_HEAD_DIM
    gm = (half[:, None] == half[None, :]).astype(np.float32)
    tri3 = np.concatenate([tri] * 3, axis=2)
    trit3 = np.concatenate([trit] * 3, axis=1)
    e01x2 = np.concatenate([e01] * 2, axis=1)
    return (jnp.asarray(tri, BF16), jnp.asarray(tri3, BF16), jnp.asarray(trit3, BF16), jnp.asarray(e01x2, BF16),
            jnp.asarray(gm, BF16))


def _pad_row(v, offset):
    v = v.reshape(1, -1).astype(F32)
    return jnp.pad(v, ((0, 0), (offset, LANES - offset - v.shape[1])))


def kernel(x, c, ctx, c_ctx, w_mod, b_mod, norm_g, w_in, ssd_conv_w, ssd_conv_b, ssd_a_log, ssd_dt_bias, ssd_d,
           ssd_norm_g, diff_qn_g, diff_kn_g, diff_lambda, diff_subln_g, ml_conv_w, ml_conv_b, ml_i_bias,
           ml_f_bias, ml_norm_g, w_branch, w_out):
    nb, seq, d = x.shape
    ctx_len = ctx.shape[1]
    depth = w_mod.shape[0]
    assert d == 1024 and ctx_len % CHUNK == 0 and seq % CHUNK == 0 and seq % GRID_W == 0 and nb < 8

    sizes = (1536, 1024, 32, 1024, 1024, 1024, 1024, 1024, 1024, 1024, 1024, 8, 8, 3072)
    offs = np.concatenate([[0], np.cumsum(sizes)])
    seg = lambda n: w_in[:, :, offs[n]:offs[n + 1]]
    w_big = jnp.concatenate([seg(1), seg(3), seg(4), seg(5), seg(6), seg(7), seg(8), seg(9), seg(10),
                             seg(13), seg(0)], axis=-1).astype(BF16)
    w_small = jnp.concatenate([seg(2), seg(11), seg(12)], axis=-1)
    w_small = jnp.pad(w_small, ((0, 0), (0, 0), (0, LANES - w_small.shape[-1]))).astype(BF16)

    cc = jnp.concatenate([c, c_ctx[None, :], jnp.zeros((8 - nb - 1, d), F32)], axis=0)
    mod = _modulation(cc, w_mod, b_mod)

    tri, tri3, trit3, e01, gm = _scan_constants()
    cos, sa, sb = _rope_tables(ctx_len, seq)
    wb = w_branch.astype(BF16)
    wo = w_out.astype(BF16)
    xall = jnp.concatenate([x, ctx], axis=1)

    for l in range(depth):
        lam_init = 0.8 - 0.6 * math.exp(-0.3 * l)
        p_big, p_small = _inproj(xall, mod[l], norm_g[l][None, :], w_big[l], w_small[l], ctx_len)
        u_ssd, u_ml = _conv(p_big, ssd_conv_w[l], ssd_conv_b[l][None, :], ml_conv_w[l], ml_conv_b[l][None, :],
                            ctx_len)
        yf, yb = _ssd(u_ssd, p_small, _pad_row(ssd_dt_bias[l], 0), _pad_row(-jnp.exp(ssd_a_log[l]), 0),
                      jnp.repeat(ssd_d[l], SSD_HEAD_DIM)[None, :], tri, tri3, trit3, e01, ctx_len)
        qn, kn = _qk_prep(p_big, cos, sa, sb, jnp.tile(diff_qn_g[l], 2)[None, :],
                          jnp.tile(diff_kn_g[l], 2)[None, :], gm)
        yd = _attention(qn, kn, p_big, diff_lambda[l], diff_qn_g[l][None, :], diff_kn_g[l][None, :],
                        diff_subln_g[l][None, :], ctx_len, lam_init)
        hf, hb = _mlstm(u_ml, p_big, p_small, _pad_row(ml_i_bias[l], SM_IG), _pad_row(ml_f_bias[l], SM_FG),
                        tri, trit3, ctx_len)
        xall = _merge(xall, mod[l], yf, yb, p_big, yd, hf, hb, ssd_norm_g[l][None, :], ml_norm_g[l][None, :],
                      wb[l], wo[l], ctx_len)
    return xall[:, :seq]
```

```python
import functools
import math

import numpy as np
import jax
import jax.numpy as jnp
from jax import lax
from jax.experimental import pallas as pl
from jax.experimental.pallas import tpu as pltpu

F32 = jnp.float32
BF16 = jnp.bfloat16

GRID_W = 64
EPS = 1e-6
CONV_K = 3
SSD_HEADS = 16
SSD_HEAD_DIM = 64
SSD_GROUPS = 2
SSD_STATE = 128
DIFF_HEADS = 8
DIFF_HEAD_DIM = 64
ROPE_BASE = 10000.0
ML_HEADS = 4
ML_QK_DIM = 128
ML_V_DIM = 256
CHUNK = 128
LANES = 128
BF16_ROWS = 16

OFF_ZS, OFF_Q, OFF_K, OFF_V, OFF_ZD = 0, 1024, 2048, 3072, 4096
OFF_MQK, OFF_MV, OFF_MO, OFF_MZ, OFF_GATES, OFF_XBC = 5120, 6144, 7168, 8192, 9216, 12288
N_BIG = 13824
SM_IG, SM_FG = 32, 40

VMEM_LIMIT = 56 * 1024 * 1024


def _cparams(sem):
    return pltpu.CompilerParams(dimension_semantics=sem, vmem_limit_bytes=VMEM_LIMIT)


def _split3(v):
    h = v.astype(BF16)
    r = v - h.astype(F32)
    m = r.astype(BF16)
    l = (r - m.astype(F32)).astype(BF16)
    return h, m, l


def _dot(a, b):
    return jnp.dot(a, b, preferred_element_type=F32)


def _dot_nt(a, b):
    return lax.dot_general(a, b, (((1,), (1,)), ((), ())), preferred_element_type=F32)


def _dot_tn(a, b):
    return lax.dot_general(a, b, (((0,), (0,)), ((), ())), preferred_element_type=F32)


def _dot_exact_rhs01(v, m01x3):
    return _dot(jnp.concatenate(_split3(v), axis=1), m01x3)


def _expand_rhs01(v, m01x2):
    h = v.astype(BF16)
    m = (v - h.astype(F32)).astype(BF16)
    return _dot(jnp.concatenate([h, m], axis=1), m01x2)


def _dot_exact_lhs01(m01x3, v):
    return _dot(m01x3, jnp.concatenate(_split3(v), axis=0))


def _sigmoid(x):
    return 1.0 / (1.0 + jnp.exp2(x * (-math.log2(math.e))))


def _silu(x):
    return x * _sigmoid(x)


def _softplus(x):
    return jnp.maximum(x, 0.0) + jnp.log(1.0 + jnp.exp(-jnp.abs(x)))


def _mod_kernel(cc_ref, w_ref, b_ref, o_ref):
    a = _silu(cc_ref[...])
    w = w_ref[0]
    ah, am, _ = _split3(a)
    wh = w.astype(BF16)
    wm = (w - wh.astype(F32)).astype(BF16)
    o_ref[0] = _dot(ah, wh) + _dot(am, wh) + _dot(ah, wm) + b_ref[0]


def _modulation(cc, w_mod, b_mod):
    depth, d, d3 = w_mod.shape
    tn = 1024
    return pl.pallas_call(
        _mod_kernel,
        grid=(depth, d3 // tn),
        in_specs=[pl.BlockSpec((8, d), lambda l, j: (0, 0)),
                  pl.BlockSpec((1, d, tn), lambda l, j: (l, 0, j)),
                  pl.BlockSpec((1, 1, tn), lambda l, j: (l, 0, j))],
        out_specs=pl.BlockSpec((1, 8, tn), lambda l, j: (l, 0, j)),
        out_shape=jax.ShapeDtypeStruct((depth, 8, d3), F32),
        compiler_params=_cparams(("arbitrary", "arbitrary")),
        name="modulation",
    )(cc, w_mod, b_mod.reshape(depth, 1, d3))


def _inproj_kernel(x_ref, mod_ref, g_ref, w_ref, ws_ref, o_ref, os_ref, h_ref, *, lat_len, ctx_row):
    b, i, j = pl.program_id(0), pl.program_id(1), pl.program_id(2)
    tm, d = h_ref.shape

    @pl.when(j == 0)
    def _():
        x = x_ref[0]
        ms = jnp.mean(x * x, axis=-1, keepdims=True)
        y = x * lax.rsqrt(ms + EPS) * g_ref[...]
        row = i * tm + lax.broadcasted_iota(jnp.int32, (tm, 1), 0)
        is_ctx = row >= lat_len
        mlat = mod_ref[pl.ds(b, 1), :]
        mctx = mod_ref[pl.ds(ctx_row, 1), :]
        shift = jnp.where(is_ctx, mctx[:, :d], mlat[:, :d])
        scale = jnp.where(is_ctx, mctx[:, d:2 * d], mlat[:, d:2 * d])
        h = (y * (1.0 + scale) + shift).astype(BF16)
        h_ref[...] = h
        os_ref[0] = _dot(h, ws_ref[...])

    o_ref[0] = _dot(h_ref[...], w_ref[...]).astype(BF16)


def _row_tile(ta, target):
    best = BF16_ROWS
    for t in range(BF16_ROWS, target + 1, BF16_ROWS):
        if ta % t == 0:
            best = t
    return best


def _inproj(xall, mod_l, g, w_big, w_small, ctx_len):
    nb, ta, d = xall.shape
    tm = _row_tile(ta, 1056)
    tn = 3456
    kern = functools.partial(_inproj_kernel, lat_len=ta - ctx_len, ctx_row=nb)
    return pl.pallas_call(
        kern,
        grid=(nb, ta // tm, N_BIG // tn),
        in_specs=[pl.BlockSpec((1, tm, d), lambda b, i, j: (b, i, 0)),
                  pl.BlockSpec(mod_l.shape, lambda b, i, j: (0, 0)),
                  pl.BlockSpec((1, d), lambda b, i, j: (0, 0)),
                  pl.BlockSpec((d, tn), lambda b, i, j: (0, j)),
                  pl.BlockSpec((d, LANES), lambda b, i, j: (0, 0))],
        out_specs=[pl.BlockSpec((1, tm, tn), lambda b, i, j: (b, i, j)),
                   pl.BlockSpec((1, tm, LANES), lambda b, i, j: (b, i, 0))],
        out_shape=[jax.ShapeDtypeStruct((nb, ta, N_BIG), BF16),
                   jax.ShapeDtypeStruct((nb, ta, LANES), F32)],
        scratch_shapes=[pltpu.VMEM((tm, d), BF16)],
        compiler_params=_cparams(("arbitrary", "arbitrary", "arbitrary")),
        name="inproj",
    )(xall, mod_l, g, w_big, w_small)


def _scan_batch(nb):
    return 4 if nb % 4 == 0 else (2 if nb % 2 == 0 else 1)


def _fwd_chunk(i, ncc, nch):
    return jnp.where(i < ncc, nch - ncc + i, i - ncc)


def _bwd_chunk(i, ncc, nch):
    del ncc
    return nch - 1 - i


def _full_spec(a):
    nd = a.ndim
    return pl.BlockSpec(a.shape, lambda b, i: (0,) * nd)


def _conv_kernel(xs_ref, xsp_ref, xsn_ref, xm_ref, xmp_ref, xmn_ref, cws_ref, cbs_ref, cwm_ref, cbm_ref,
                 us_ref, um_ref, *, lat_len, ta):
    tm = xs_ref.shape[1]
    row0 = pl.program_id(1) * tm
    has_prev = jnp.logical_and(row0 != 0, row0 != lat_len)
    has_next = jnp.logical_and(row0 + tm != lat_len, row0 + tm != ta)
    rid = lax.broadcasted_iota(jnp.int32, (tm, LANES), 0)
    for x_ref, xp_ref, xn_ref, cw_ref, cb_ref, o_ref in ((xs_ref, xsp_ref, xsn_ref, cws_ref, cbs_ref, us_ref),
                                                         (xm_ref, xmp_ref, xmn_ref, cwm_ref, cbm_ref, um_ref)):
        for j in range(x_ref.shape[2] // LANES):
            sl = slice(j * LANES, (j + 1) * LANES)
            x = x_ref[0, :, sl].astype(F32)
            prow = jnp.where(has_prev, xp_ref[0, BF16_ROWS - 1:BF16_ROWS, sl].astype(F32), 0.0)
            nrow = jnp.where(has_next, xn_ref[0, 0:1, sl].astype(F32), 0.0)
            xprev = jnp.where(rid == 0, prow, pltpu.roll(x, 1, 0))
            xnext = jnp.where(rid == tm - 1, nrow, pltpu.roll(x, tm - 1, 0))
            u = xprev * cw_ref[0:1, sl] + x * cw_ref[1:2, sl] + xnext * cw_ref[2:3, sl] + cb_ref[:, sl]
            o_ref[0, :, sl] = _silu(u).astype(BF16)


def _conv(p_big, cws, cbs, cwm, cbm, ctx_len):
    nb, ta, _ = p_big.shape
    tm = _row_tile(math.gcd(ta - ctx_len, ctx_len), 256)
    per = tm // BF16_ROWS
    nrow16 = ta // BF16_ROWS
    ws, wm = cws.shape[1], cwm.shape[1]

    def specs(width, col_block):
        return [pl.BlockSpec((1, tm, width), lambda b, i: (b, i, col_block)),
                pl.BlockSpec((1, BF16_ROWS, width), lambda b, i: (b, jnp.maximum(i * per - 1, 0), col_block)),
                pl.BlockSpec((1, BF16_ROWS, width),
                             lambda b, i: (b, jnp.minimum((i + 1) * per, nrow16 - 1), col_block))]

    consts = [cws, cbs, cwm, cbm]
    return pl.pallas_call(
        functools.partial(_conv_kernel, lat_len=ta - ctx_len, ta=ta),
        grid=(nb, ta // tm),
        in_specs=specs(ws, OFF_XBC // ws) + specs(wm, OFF_MQK // wm) + [_full_spec(a) for a in consts],
        out_specs=[pl.BlockSpec((1, tm, ws), lambda b, i: (b, i, 0)),
                   pl.BlockSpec((1, tm, wm), lambda b, i: (b, i, 0))],
        out_shape=[jax.ShapeDtypeStruct((nb, ta, ws), BF16), jax.ShapeDtypeStruct((nb, ta, wm), BF16)],
        compiler_params=_cparams(("arbitrary", "arbitrary")),
        name="conv_silu",
    )(p_big, p_big, p_big, p_big, p_big, p_big, *consts)


def _ssd_decay_sums(d, ps, dtb_ref, a_ref, tri3_ref, trit3_ref):
    dt = _softplus(ps + dtb_ref[...])
    a = dt * a_ref[...]
    acs = _dot_exact_lhs01(tri3_ref[d], a)
    acs_t = _dot_exact_rhs01(a.T, trit3_ref[d])
    return dict(dt=dt, acs=acs, acs_t=acs_t)


def _ssd_expansions(d, sd, e_ref):
    L = CHUNK
    last = L - 1 if d == 0 else 0
    dend = jnp.exp(sd["acs"][last:last + 1, :] - sd["acs"])
    e01 = e_ref[d]
    eacs_e = _expand_rhs01(jnp.exp(sd["acs"]), e01)
    sd.update(dt_e=_expand_rhs01(sd["dt"], e01), dtend_e=_expand_rhs01(sd["dt"] * dend, e01),
              eacs_e=eacs_e, cd_e=eacs_e[last:last + 1, :])


def _ssd_kernel(uf, ub, psf, psb, dtb, aneg, dsk, tri, tri3, trit3, e01, yf, yb, s_ref):
    @pl.when(pl.program_id(1) == 0)
    def _():
        s_ref[...] = jnp.zeros_like(s_ref)

    L = CHUNK
    hp = SSD_HEADS * SSD_HEAD_DIM
    gn = SSD_GROUPS * SSD_STATE
    gw = hp // SSD_GROUPS
    hpg = SSD_HEADS // SSD_GROUPS
    u_refs, y_refs, ps_refs = (uf, ub), (yf, yb), (psf, psb)
    bds = [(b, d) for b in range(uf.shape[0]) for d in range(2)]
    dgs = [(b, d, g) for b, d in bds for g in range(SSD_GROUPS)]

    cg, bg, sg, yo = {}, {}, {}, {}
    for b, d, g in dgs:
        bg[b, d, g] = u_refs[d][b, :, hp + g * SSD_STATE:hp + (g + 1) * SSD_STATE]
        cg[b, d, g] = u_refs[d][b, :, hp + gn + g * SSD_STATE:hp + gn + (g + 1) * SSD_STATE]
        sg[b, d, g] = _dot_nt(cg[b, d, g], bg[b, d, g])
        yo[b, d, g] = _dot(cg[b, d, g], s_ref[b, d, g].astype(BF16))
    st = {(b, d): _ssd_decay_sums(d, ps_refs[d][b], dtb, aneg, tri3, trit3) for b, d in bds}
    for b, d in bds:
        _ssd_expansions(d, st[b, d], e01)
    xs = {(b, d): u_refs[d][b, :, :hp].astype(F32) for b, d in bds}
    masks = [tri[d] > 0 for d in range(2)]
    left = lax.broadcasted_iota(jnp.int32, (L, LANES), 1) < SSD_HEAD_DIM

    pairs, snew = {}, {}
    for b, d, g in dgs:
        sd = st[b, d]
        xd = xs[b, d][:, g * gw:(g + 1) * gw] * sd["dt_e"][:, g * gw:(g + 1) * gw]
        for k in range(hpg // 2):
            ms = []
            for h in (g * hpg + 2 * k, g * hpg + 2 * k + 1):
                j = SSD_HEADS * d + h
                seg = sd["acs"][:, j:j + 1] - sd["acs_t"][j:j + 1, :]
                ms.append((sg[b, d, g] * jnp.exp(jnp.where(masks[d], seg, -jnp.inf))).astype(BF16))
            xp = xd[:, 2 * k * SSD_HEAD_DIM:(2 * k + 2) * SSD_HEAD_DIM]
            rhs = jnp.concatenate([jnp.where(left, xp, 0.0), jnp.where(left, 0.0, xp)], axis=0).astype(BF16)
            pairs[b, d, g, k] = _dot(jnp.concatenate(ms, axis=1), rhs)
        xde = (xs[b, d][:, g * gw:(g + 1) * gw] * sd["dtend_e"][:, g * gw:(g + 1) * gw]).astype(BF16)
        snew[b, d, g] = _dot_tn(bg[b, d, g], xde)

    for b, d in bds:
        ys = []
        for g in range(SSD_GROUPS):
            sl = slice(g * gw, (g + 1) * gw)
            ys.append(jnp.concatenate([pairs[b, d, g, k] for k in range(hpg // 2)], axis=1)
                      + yo[b, d, g] * st[b, d]["eacs_e"][:, sl])
            s_ref[b, d, g] = s_ref[b, d, g] * st[b, d]["cd_e"][:, sl] + snew[b, d, g]
        y = jnp.concatenate(ys, axis=1)
        if d == 0:
            y = y + dsk[...] * xs[b, d]
        y_refs[d][0, b] = y.astype(BF16)


def _ssd(u_ssd, p_small, dtb, aneg, dsk, tri, tri3, trit3, e01, ctx_len):
    nb, ta, width = u_ssd.shape
    ncc, nch = ctx_len // CHUNK, ta // CHUNK
    hp = SSD_HEADS * SSD_HEAD_DIM
    bb = _scan_batch(nb)
    xspecs = [pl.BlockSpec((bb, CHUNK, width), lambda b, i: (b, _fwd_chunk(i, ncc, nch), 0)),
              pl.BlockSpec((bb, CHUNK, width), lambda b, i: (b, _bwd_chunk(i, ncc, nch), 0))]
    ps_f = pl.BlockSpec((bb, CHUNK, LANES), lambda b, i: (b, _fwd_chunk(i, ncc, nch), 0))
    ps_b = pl.BlockSpec((bb, CHUNK, LANES), lambda b, i: (b, _bwd_chunk(i, ncc, nch), 0))
    consts = [dtb, aneg, dsk, tri, tri3, trit3, e01]
    yf, yb = pl.pallas_call(
        _ssd_kernel,
        grid=(nb // bb, nch),
        in_specs=xspecs + [ps_f, ps_b] + [_full_spec(a) for a in consts],
        out_specs=[pl.BlockSpec((1, bb, CHUNK, hp), lambda b, i: (0, b, _fwd_chunk(i, ncc, nch), 0)),
                   pl.BlockSpec((1, bb, CHUNK, hp), lambda b, i: (0, b, _bwd_chunk(i, ncc, nch), 0))],
        out_shape=[jax.ShapeDtypeStruct((1, nb, ta, hp), BF16)] * 2,
        scratch_shapes=[pltpu.VMEM((bb, 2, SSD_GROUPS, SSD_STATE, hp // SSD_GROUPS), F32)],
        compiler_params=_cparams(("arbitrary", "arbitrary")),
        name="ssd_scan",
    )(u_ssd, u_ssd, p_small, p_small, *consts)
    return yf[0], yb[0]


def _ml_stats(bds, ps_refs, ib_ref, fb_ref, trit3_ref, m_ref):
    L = CHUNK
    lane = lax.broadcasted_iota(jnp.int32, (8, L), 1)
    st = {}
    for b, d in bds:
        ps = ps_refs[d][b]
        li_t = (ps + ib_ref[...]).T
        lf_t = (-_softplus(-(ps + fb_ref[...]))).T
        bc = _dot_exact_rhs01(lf_t, trit3_ref[d])[SM_FG:SM_FG + 8, :]
        st[b, d] = dict(bc=bc, lir=li_t[SM_IG:SM_IG + 8, :])
    for b, d in bds:
        sd = st[b, d]
        last = L - 1 if d == 0 else 0
        m_prev = m_ref[b, d]
        b_last = jnp.broadcast_to(sd["bc"][:, last:last + 1], (8, L))
        gl = b_last - sd["bc"] + sd["lir"]
        m_loc = jnp.broadcast_to(jnp.max(gl, axis=-1, keepdims=True), (8, L))
        m_new = jnp.maximum(b_last + m_prev, m_loc)
        sd.update(w=jnp.exp(gl - m_loc), a_s=jnp.exp(b_last + m_prev - m_new), s_s=jnp.exp(m_loc - m_new),
                  m_new=m_new, rowb=sd["bc"] - sd["lir"], g=sd["lir"] - sd["bc"], e=sd["bc"] + m_prev)
    sh = 1
    while sh < L:
        for b, d in bds:
            g = st[b, d]["g"]
            if d == 0:
                st[b, d]["g"] = jnp.maximum(g, jnp.where(lane >= sh, pltpu.roll(g, sh, 1), -jnp.inf))
            else:
                st[b, d]["g"] = jnp.maximum(g, jnp.where(lane < L - sh, pltpu.roll(g, L - sh, 1), -jnp.inf))
        sh *= 2
    for b, d in bds:
        sd = st[b, d]
        m_t = jnp.maximum(sd["e"], sd["bc"] + sd["g"])
        sd["cols"] = jnp.concatenate([sd["bc"] - m_t, jnp.exp(sd["e"] - m_t), jnp.exp(-m_t),
                                      jnp.zeros((L - 24, L), F32)], axis=0).T
    return st


def _ml_kernel(uf, ub, vf, vb, psf, psb, ib, fb, tri, trit3, hf, hb, c_ref, n_ref, m_ref):
    @pl.when(pl.program_id(1) == 0)
    def _():
        c_ref[...] = jnp.zeros_like(c_ref)
        n_ref[...] = jnp.zeros_like(n_ref)
        m_ref[...] = jnp.zeros_like(m_ref)

    L = CHUNK
    hq = ML_HEADS * ML_QK_DIM
    rep = ML_V_DIM // LANES
    ones = jnp.ones((L, LANES), BF16)
    u_refs, v_refs, h_refs, ps_refs = (uf, ub), (vf, vb), (hf, hb), (psf, psb)
    masks = [tri[d] > 0 for d in range(2)]
    bds = [(b, d) for b in range(uf.shape[0]) for d in range(2)]
    units = [(b, d, h) for b, d in bds for h in range(ML_HEADS)]

    st1 = {}
    for b, d, h in units:
        q = u_refs[d][b, :, h * ML_QK_DIM:(h + 1) * ML_QK_DIM]
        k = u_refs[d][b, :, hq + h * ML_QK_DIM:hq + (h + 1) * ML_QK_DIM].astype(F32) * (ML_QK_DIM ** -0.5)
        st1[b, d, h] = dict(k=k, s=_dot_nt(q, k.astype(BF16)), qc=_dot(q, c_ref[b, d, h].astype(BF16)),
                            qn=_dot(q, n_ref[b, d, h].astype(BF16)))
    stats = _ml_stats(bds, ps_refs, ib, fb, trit3, m_ref)
    for b, d, h in units:
        r = ML_HEADS * d + h
        v = v_refs[d][b, :, h * ML_V_DIM:(h + 1) * ML_V_DIM]
        kwt = (st1[b, d, h]["k"].T * stats[b, d]["w"][r:r + 1, :]).astype(BF16)
        st1[b, d, h].update(v=v, c_loc=_dot(kwt, v), n_loc=_dot(kwt, ones))
    st2 = {}
    for b, d, h in units:
        r = ML_HEADS * d + h
        cola = stats[b, d]["cols"][:, r:r + 1]
        x = jnp.exp(jnp.where(masks[d], cola - stats[b, d]["rowb"][r:r + 1, :], -jnp.inf))
        wts = (x * st1[b, d, h]["s"]).astype(BF16)
        st2[b, d, h] = (_dot(wts, st1[b, d, h]["v"]), _dot(wts, ones))
    for b, d in bds:
        outs = []
        for h in range(ML_HEADS):
            r = ML_HEADS * d + h
            s1 = st1[b, d, h]
            sc = stats[b, d]["cols"][:, 8 + r:9 + r]
            emt = stats[b, d]["cols"][:, 16 + r:17 + r]
            wv, wo = st2[b, d, h]
            num = wv + s1["qc"] * sc
            den = wo + s1["qn"] * sc
            rden = 1.0 / jnp.maximum(jnp.abs(den), emt)
            outs.append(num * jnp.concatenate([rden] * rep, axis=1))
            a_r = stats[b, d]["a_s"][r:r + 1, :]
            s_r = stats[b, d]["s_s"][r:r + 1, :]
            c_ref[b, d, h] = (jnp.concatenate([a_r] * rep, axis=1) * c_ref[b, d, h]
                              + jnp.concatenate([s_r] * rep, axis=1) * s1["c_loc"])
            n_ref[b, d, h] = a_r * n_ref[b, d, h] + s_r * s1["n_loc"]
        h_refs[d][0, b] = jnp.concatenate(outs, axis=1).astype(BF16)
        m_ref[b, d] = stats[b, d]["m_new"]


def _mlstm(u_ml, p_big, p_small, ib, fb, tri, trit3, ctx_len):
    nb, ta, wq = u_ml.shape
    ncc, nch = ctx_len // CHUNK, ta // CHUNK
    wv = ML_HEADS * ML_V_DIM
    bb = _scan_batch(nb)
    qspecs = [pl.BlockSpec((bb, CHUNK, wq), lambda b, i: (b, _fwd_chunk(i, ncc, nch), 0)),
              pl.BlockSpec((bb, CHUNK, wq), lambda b, i: (b, _bwd_chunk(i, ncc, nch), 0))]
    v_f = pl.BlockSpec((bb, CHUNK, wv), lambda b, i: (b, _fwd_chunk(i, ncc, nch), OFF_MV // wv))
    v_b = pl.BlockSpec((bb, CHUNK, wv), lambda b, i: (b, _bwd_chunk(i, ncc, nch), OFF_MV // wv))
    ps_f = pl.BlockSpec((bb, CHUNK, LANES), lambda b, i: (b, _fwd_chunk(i, ncc, nch), 0))
    ps_b = pl.BlockSpec((bb, CHUNK, LANES), lambda b, i: (b, _bwd_chunk(i, ncc, nch), 0))
    consts = [ib, fb, tri, trit3]
    hf, hb = pl.pallas_call(
        _ml_kernel,
        grid=(nb // bb, nch),
        in_specs=qspecs + [v_f, v_b, ps_f, ps_b] + [_full_spec(a) for a in consts],
        out_specs=[pl.BlockSpec((1, bb, CHUNK, wv), lambda b, i: (0, b, _fwd_chunk(i, ncc, nch), 0)),
                   pl.BlockSpec((1, bb, CHUNK, wv), lambda b, i: (0, b, _bwd_chunk(i, ncc, nch), 0))],
        out_shape=[jax.ShapeDtypeStruct((1, nb, ta, wv), BF16)] * 2,
        scratch_shapes=[pltpu.VMEM((bb, 2, ML_HEADS, ML_QK_DIM, ML_V_DIM), F32),
                        pltpu.VMEM((bb, 2, ML_HEADS, ML_QK_DIM, LANES), F32),
                        pltpu.VMEM((bb, 2, 8, CHUNK), F32)],
        compiler_params=_cparams(("arbitrary", "arbitrary")),
        name="mlstm_scan",
    )(u_ml, u_ml, p_big, p_big, p_small, p_small, *consts)
    return hf[0], hb[0]


def _qk_prep_kernel(q_ref, k_ref, cos_ref, sa_ref, sb_ref, gq_ref, gk_ref, gm_ref, qo_ref, ko_ref, *, qscale):
    gm = gm_ref[...]
    nheads = q_ref.shape[2] // LANES
    tm = q_ref.shape[1]
    rb = _row_tile(tm, 192)

    def prep(t, g, mult, cos, sa, sb):
        t = t.astype(F32)
        sq = t * t
        sh = sq.astype(BF16)
        sm = (sq - sh.astype(F32)).astype(BF16)
        ms = (_dot(sh, gm) + _dot(sm, gm)) * (1.0 / DIFF_HEAD_DIM)
        y = t * lax.rsqrt(ms + EPS) * g
        r = y * cos + pltpu.roll(y, LANES - 16, 1) * sa + pltpu.roll(y, 16, 1) * sb
        return (r * mult).astype(BF16)

    for r in range(tm // rb):
        rows = slice(r * rb, (r + 1) * rb)
        tabs = (cos_ref[rows, :], sa_ref[rows, :], sb_ref[rows, :])
        for h in range(nheads):
            sl = slice(h * LANES, (h + 1) * LANES)
            qo_ref[0, rows, sl] = prep(q_ref[0, rows, sl], gq_ref[...], qscale, *tabs)
            ko_ref[0, rows, sl] = prep(k_ref[0, rows, sl], gk_ref[...], 1.0, *tabs)


def _qk_prep(p_big, cos, sa, sb, gq, gk, gm):
    nb, ta, _ = p_big.shape
    w = DIFF_HEADS * 2 * DIFF_HEAD_DIM
    tm = _row_tile(ta, 1056)
    qscale = DIFF_HEAD_DIM ** -0.5 * math.log2(math.e)
    row = lambda b, i: (i, 0)
    return pl.pallas_call(
        functools.partial(_qk_prep_kernel, qscale=qscale),
        grid=(nb, ta // tm),
        in_specs=[pl.BlockSpec((1, tm, w), lambda b, i: (b, i, OFF_Q // w)),
                  pl.BlockSpec((1, tm, w), lambda b, i: (b, i, OFF_K // w)),
                  pl.BlockSpec((tm, LANES), row), pl.BlockSpec((tm, LANES), row), pl.BlockSpec((tm, LANES), row),
                  _full_spec(gq), _full_spec(gk), _full_spec(gm)],
        out_specs=[pl.BlockSpec((1, tm, w), lambda b, i: (b, i, 0))] * 2,
        out_shape=[jax.ShapeDtypeStruct((nb, ta, w), BF16)] * 2,
        compiler_params=_cparams(("arbitrary", "arbitrary")),
        name="qk_prep",
    )(p_big, p_big, cos, sa, sb, gq, gk, gm)


def _attn_kernel(q_ref, k_ref, v_ref, z_ref, lam_ref, gq_ref, gk_ref, sg_ref, *rest, tk, lam_init):
    o_ref, vp_ref, e_ref = rest[-3:]
    qi = pl.program_id(2)
    nkeys = k_ref.shape[1]
    dh = DIFF_HEAD_DIM

    @pl.when(qi == 0)
    def _():
        vp_ref[:, :LANES] = v_ref[0]
        vp_ref[:, LANES:] = jnp.ones((nkeys, LANES), BF16)

    lp = lam_ref[...]
    lam = (jnp.exp(jnp.sum(lp[0:1] * lp[1:2], axis=-1, keepdims=True))
           - jnp.exp(jnp.sum(lp[2:3] * lp[3:4], axis=-1, keepdims=True)) + lam_init)
    shift = (jnp.max(jnp.abs(gq_ref[...]), axis=-1, keepdims=True)
             * jnp.max(jnp.abs(gk_ref[...]), axis=-1, keepdims=True)
             * (dh * dh ** -0.5 * math.log2(math.e)))

    q = q_ref[0]
    lane = lax.broadcasted_iota(jnp.int32, q.shape, 1)
    zero = jnp.zeros_like(q)
    acc = []
    for c, qc in enumerate((jnp.where(lane < dh, q, zero), jnp.where(lane < dh, zero, q))):
        for j in range(nkeys // tk):
            s = _dot_nt(qc, k_ref[0, j * tk:(j + 1) * tk, :])
            e_ref[c, :, j * tk:(j + 1) * tk] = jnp.exp2(s - shift).astype(BF16)
        acc.append(_dot(e_ref[c], vp_ref[...]))
    a0, a1 = acc
    o = a0[:, :LANES] / a0[:, LANES:] - lam * (a1[:, :LANES] / a1[:, LANES:])
    ms = jnp.mean(o * o, axis=-1, keepdims=True)
    o = o * lax.rsqrt(ms + EPS) * sg_ref[...] * (1.0 - lam_init)
    o_ref[0] = (o * _silu(z_ref[0].astype(F32))).astype(BF16)


def _attention(qn, kn, p_big, lam_p, gq, gk, sg, ctx_len, lam_init):
    nb, ta, w = qn.shape
    seq = ta - ctx_len
    tk = 256
    tq = 1024 if seq % 1024 == 0 else 256
    assert seq % tq == 0 and seq % ctx_len == 0 and ta % tk == 0 and ctx_len % tk == 0
    kern = functools.partial(_attn_kernel, tk=tk, lam_init=lam_init)
    full = lambda a: pl.BlockSpec(a.shape, lambda b, h, i: (0,) * a.ndim)
    consts = [lam_p, gq, gk, sg]

    def call(tq_, nkeys, row_block, key_block, nq, prev):
        in_specs = [pl.BlockSpec((1, tq_, LANES), lambda b, h, i: (b, row_block + i, h)),
                    pl.BlockSpec((1, nkeys, LANES), lambda b, h, i: (b, key_block, h)),
                    pl.BlockSpec((1, nkeys, LANES), lambda b, h, i: (b, key_block, OFF_V // LANES + h)),
                    pl.BlockSpec((1, tq_, LANES), lambda b, h, i: (b, row_block + i, OFF_ZD // LANES + h))]
        in_specs += [full(a) for a in consts]
        args = [qn, kn, p_big, p_big] + consts
        aliases = {}
        if prev is not None:
            in_specs.append(pl.BlockSpec(memory_space=pl.ANY))
            aliases = {len(args): 0}
            args.append(prev)
        return pl.pallas_call(
            kern,
            grid=(nb, DIFF_HEADS, nq),
            in_specs=in_specs,
            out_specs=pl.BlockSpec((1, tq_, LANES), lambda b, h, i: (b, row_block + i, h)),
            out_shape=jax.ShapeDtypeStruct((nb, ta, w), BF16),
            scratch_shapes=[pltpu.VMEM((nkeys, 2 * LANES), BF16), pltpu.VMEM((2, tq_, nkeys), BF16)],
            input_output_aliases=aliases,
            compiler_params=_cparams(("arbitrary", "arbitrary", "arbitrary")),
            name="diff_attention" if prev is None else "diff_attention_ctx",
        )(*args)

    yd = call(tq, ta, 0, 0, seq // tq, None)
    return call(ctx_len, ctx_len, seq // ctx_len, seq // ctx_len, 1, yd)


def _merge_kernel(x_ref, mod_ref, yf_ref, yb_ref, zs_ref, yd_ref, hf_ref, hb_ref, mo_ref, mz_ref, gt_ref,
                  sng_ref, mng_ref, wb_ref, wo_ref, o_ref, *, lat_len, ctx_row):
    b, i = pl.program_id(0), pl.program_id(1)
    tm, d = x_ref.shape[1], x_ref.shape[2]

    pd = _dot(yd_ref[0], wb_ref[1])
    ya = (yf_ref[0].astype(F32) + yb_ref[0].astype(F32)) * _silu(zs_ref[0].astype(F32))
    ya = ya * lax.rsqrt(jnp.mean(ya * ya, axis=-1, keepdims=True) + EPS) * sng_ref[...]
    pa = _dot(ya.astype(BF16), wb_ref[0])

    hm = (hf_ref[0].astype(F32) + hb_ref[0].astype(F32)) * _sigmoid(mo_ref[0].astype(F32))
    parts = []
    for h in range(ML_HEADS):
        t = hm[:, h * ML_V_DIM:(h + 1) * ML_V_DIM]
        parts.append(t * lax.rsqrt(jnp.mean(t * t, axis=-1, keepdims=True) + EPS))
    yc = jnp.concatenate(parts, axis=1) * mng_ref[...] * _silu(mz_ref[0].astype(F32))

    g = _sigmoid(gt_ref[0].astype(F32))
    mixed = (g[:, :d] * pa + g[:, d:2 * d] * pd + g[:, 2 * d:] * _dot(yc.astype(BF16), wb_ref[2]))
    out = _dot(mixed.astype(BF16), wo_ref[...])

    row = i * tm + lax.broadcasted_iota(jnp.int32, (tm, 1), 0)
    is_ctx = row >= lat_len
    gate = jnp.where(is_ctx, mod_ref[pl.ds(ctx_row, 1), 2 * d:], mod_ref[pl.ds(b, 1), 2 * d:])
    o_ref[0] = x_ref[0] + gate * out


def _merge(xall, mod_l, yf, yb, p_big, yd, hf, hb, sng, mng, wb, wo, ctx_len, out_rows):
    nb, ta, d = xall.shape
    tm = _row_tile(out_rows, 352)
    kern = functools.partial(_merge_kernel, lat_len=ta - ctx_len, ctx_row=nb)
    blk = lambda col: pl.BlockSpec((1, tm, d), lambda b, i: (b, i, col))
    full = lambda a: pl.BlockSpec(a.shape, lambda b, i: (0,) * a.ndim)
    return pl.pallas_call(
        kern,
        grid=(nb, out_rows // tm),
        in_specs=[blk(0), full(mod_l), blk(0), blk(0), blk(OFF_ZS // d), blk(0), blk(0), blk(0),
                  blk(OFF_MO // d), blk(OFF_MZ // d),
                  pl.BlockSpec((1, tm, 3 * d), lambda b, i: (b, i, OFF_GATES // (3 * d))),
                  full(sng), full(mng), full(wb), full(wo)],
        out_specs=blk(0),
        out_shape=jax.ShapeDtypeStruct((nb, out_rows, d), F32),
        compiler_params=_cparams(("arbitrary", "arbitrary")),
        name="merge",
    )(xall, mod_l, yf, yb, p_big, yd, hf, hb, p_big, p_big, p_big, sng, mng, wb, wo)


def _rope_tables(ctx_len, seq):
    rows = seq // GRID_W
    row = jnp.repeat(jnp.arange(rows, dtype=F32), GRID_W)
    col = jnp.tile(jnp.arange(GRID_W, dtype=F32), rows)
    half = DIFF_HEAD_DIM // 2
    inv_freq = ROPE_BASE ** (-jnp.arange(0, half, 2, dtype=F32) / half)
    ang_r = row[:, None] * inv_freq
    ang_c = col[:, None] * inv_freq
    cos = jnp.concatenate([jnp.cos(ang_r), jnp.cos(ang_r), jnp.cos(ang_c), jnp.cos(ang_c)], axis=-1)
    sin = jnp.concatenate([jnp.sin(ang_r), jnp.sin(ang_r), jnp.sin(ang_c), jnp.sin(ang_c)], axis=-1)
    cos = jnp.concatenate([cos, jnp.ones((ctx_len, DIFF_HEAD_DIM), F32)], axis=0)
    sin = jnp.concatenate([sin, jnp.zeros((ctx_len, DIFF_HEAD_DIM), F32)], axis=0)
    cos = jnp.tile(cos, (1, LANES // DIFF_HEAD_DIM))
    sin = jnp.tile(sin, (1, LANES // DIFF_HEAD_DIM))
    first = (jnp.arange(LANES) % (DIFF_HEAD_DIM // 2)) < DIFF_HEAD_DIM // 4
    return cos, jnp.where(first, -sin, 0.0), jnp.where(first, 0.0, sin)


def _scan_constants():
    idx = np.arange(CHUNK)
    tri_f = (idx[None, :] <= idx[:, None]).astype(np.float32)
    tri = np.stack([tri_f, tri_f.T])
    trit = np.stack([tri_f.T, tri_f])
    e01 = np.zeros((2, LANES, SSD_HEADS * SSD_HEAD_DIM), np.float32)
    for d in range(2):
        for h in range(SSD_HEADS):
            e01[d, SSD_HEADS * d + h, h * SSD_HEAD_DIM:(h + 1) * SSD_HEAD_DIM] = 1.0
    half = np.arange(LANES) // DIFF_HEAD_DIM
    gm = (half[:, None] == half[None, :]).astype(np.float32)
    tri3 = np.concatenate([tri] * 3, axis=2)
    trit3 = np.concatenate([trit] * 3, axis=1)
    e01x2 = np.concatenate([e01] * 2, axis=1)
    return (jnp.asarray(tri, BF16), jnp.asarray(tri3, BF16), jnp.asarray(trit3, BF16), jnp.asarray(e01x2, BF16),
            jnp.asarray(gm, BF16))


def _pad_row(v, offset):
    v = v.reshape(1, -1).astype(F32)
    return jnp.pad(v, ((0, 0), (offset, LANES - offset - v.shape[1])))


def kernel(x, c, ctx, c_ctx, w_mod, b_mod, norm_g, w_in, ssd_conv_w, ssd_conv_b, ssd_a_log, ssd_dt_bias, ssd_d,
           ssd_norm_g, diff_qn_g, diff_kn_g, diff_lambda, diff_subln_g, ml_conv_w, ml_conv_b, ml_i_bias,
           ml_f_bias, ml_norm_g, w_branch, w_out):
    nb, seq, d = x.shape
    ctx_len = ctx.shape[1]
    depth = w_mod.shape[0]
    assert d == 1024 and ctx_len % CHUNK == 0 and seq % CHUNK == 0 and seq % GRID_W == 0 and nb < 8

    sizes = (1536, 1024, 32, 1024, 1024, 1024, 1024, 1024, 1024, 1024, 1024, 8, 8, 3072)
    offs = np.concatenate([[0], np.cumsum(sizes)])
    seg = lambda n: w_in[:, :, offs[n]:offs[n + 1]]
    w_big = jnp.concatenate([seg(1), seg(3), seg(4), seg(5), seg(6), seg(7), seg(8), seg(9), seg(10),
                             seg(13), seg(0)], axis=-1).astype(BF16)
    w_small = jnp.concatenate([seg(2), seg(11), seg(12)], axis=-1)
    w_small = jnp.pad(w_small, ((0, 0), (0, 0), (0, LANES - w_small.shape[-1]))).astype(BF16)

    cc = jnp.concatenate([c, c_ctx[None, :], jnp.zeros((8 - nb - 1, d), F32)], axis=0)
    mod = _modulation(cc, w_mod, b_mod)

    tri, tri3, trit3, e01, gm = _scan_constants()
    cos, sa, sb = _rope_tables(ctx_len, seq)
    wb = w_branch.astype(BF16)
    wo = w_out.astype(BF16)
    xall = jnp.concatenate([x, ctx], axis=1)
    ta = seq + ctx_len

    for l in range(depth):
        lam_init = 0.8 - 0.6 * math.exp(-0.3 * l)
        p_big, p_small = _inproj(xall, mod[l], norm_g[l][None, :], w_big[l], w_small[l], ctx_len)
        u_ssd, u_ml = _conv(p_big, ssd_conv_w[l], ssd_conv_b[l][None, :], ml_conv_w[l], ml_conv_b[l][None, :],
                            ctx_len)
        yf, yb = _ssd(u_ssd, p_small, _pad_row(ssd_dt_bias[l], 0), _pad_row(-jnp.exp(ssd_a_log[l]), 0),
                      jnp.repeat(ssd_d[l], SSD_HEAD_DIM)[None, :], tri, tri3, trit3, e01, ctx_len)
        qn, kn = _qk_prep(p_big, cos, sa, sb, jnp.tile(diff_qn_g[l], 2)[None, :],
                          jnp.tile(diff_kn_g[l], 2)[None, :], gm)
        yd = _attention(qn, kn, p_big, diff_lambda[l], diff_qn_g[l][None, :], diff_kn_g[l][None, :],
                        diff_subln_g[l][None, :], ctx_len, lam_init)
        hf, hb = _mlstm(u_ml, p_big, p_small, _pad_row(ml_i_bias[l], SM_IG), _pad_row(ml_f_bias[l], SM_FG),
                        tri, trit3, ctx_len)
        xall = _merge(xall, mod[l], yf, yb, p_big, yd, hf, hb, ssd_norm_g[l][None, :], ml_norm_g[l][None, :],
                      wb[l], wo[l], ctx_len, ta if l + 1 < depth else seq)
    return xall
```

```python
import functools
import math

import numpy as np
import jax
import jax.numpy as jnp
from jax import lax
from jax.experimental import pallas as pl
from jax.experimental.pallas import tpu as pltpu

F32 = jnp.float32
BF16 = jnp.bfloat16

GRID_W = 64
EPS = 1e-6
CONV_K = 3
SSD_HEADS = 16
SSD_HEAD_DIM = 64
SSD_GROUPS = 2
SSD_STATE = 128
DIFF_HEADS = 8
DIFF_HEAD_DIM = 64
ROPE_BASE = 10000.0
ML_HEADS = 4
ML_QK_DIM = 128
ML_V_DIM = 256
CHUNK = 128
LANES = 128
BF16_ROWS = 16

OFF_ZS, OFF_Q, OFF_K, OFF_V, OFF_ZD = 0, 1024, 2048, 3072, 4096
OFF_MQK, OFF_MV, OFF_MO, OFF_MZ, OFF_GATES, OFF_XBC = 5120, 6144, 7168, 8192, 9216, 12288
N_BIG = 13824
SM_IG, SM_FG = 32, 40

VMEM_LIMIT = 56 * 1024 * 1024


def _cparams(sem):
    return pltpu.CompilerParams(dimension_semantics=sem, vmem_limit_bytes=VMEM_LIMIT)


def _split3(v):
    h = v.astype(BF16)
    r = v - h.astype(F32)
    m = r.astype(BF16)
    l = (r - m.astype(F32)).astype(BF16)
    return h, m, l


def _dot(a, b):
    return jnp.dot(a, b, preferred_element_type=F32)


def _dot_nt(a, b):
    return lax.dot_general(a, b, (((1,), (1,)), ((), ())), preferred_element_type=F32)


def _dot_tn(a, b):
    return lax.dot_general(a, b, (((0,), (0,)), ((), ())), preferred_element_type=F32)


def _dot_exact_rhs01(v, m01x3):
    return _dot(jnp.concatenate(_split3(v), axis=1), m01x3)


def _expand_rhs01(v, m01x2):
    h = v.astype(BF16)
    m = (v - h.astype(F32)).astype(BF16)
    return _dot(jnp.concatenate([h, m], axis=1), m01x2)


def _dot_exact_lhs01(m01x3, v):
    return _dot(m01x3, jnp.concatenate(_split3(v), axis=0))


def _sigmoid(x):
    return 1.0 / (1.0 + jnp.exp2(x * (-math.log2(math.e))))


def _silu(x):
    return x * _sigmoid(x)


def _softplus(x):
    return jnp.maximum(x, 0.0) + jnp.log(1.0 + jnp.exp(-jnp.abs(x)))


def _mod_kernel(cc_ref, w_ref, b_ref, o_ref):
    a = _silu(cc_ref[...])
    w = w_ref[0]
    ah, am, _ = _split3(a)
    wh = w.astype(BF16)
    wm = (w - wh.astype(F32)).astype(BF16)
    o_ref[0] = _dot(ah, wh) + _dot(am, wh) + _dot(ah, wm) + b_ref[0]


def _modulation(cc, w_mod, b_mod):
    depth, d, d3 = w_mod.shape
    tn = 1024
    return pl.pallas_call(
        _mod_kernel,
        grid=(depth, d3 // tn),
        in_specs=[pl.BlockSpec((8, d), lambda l, j: (0, 0)),
                  pl.BlockSpec((1, d, tn), lambda l, j: (l, 0, j)),
                  pl.BlockSpec((1, 1, tn), lambda l, j: (l, 0, j))],
        out_specs=pl.BlockSpec((1, 8, tn), lambda l, j: (l, 0, j)),
        out_shape=jax.ShapeDtypeStruct((depth, 8, d3), F32),
        compiler_params=_cparams(("arbitrary", "arbitrary")),
        name="modulation",
    )(cc, w_mod, b_mod.reshape(depth, 1, d3))


def _inproj_kernel(x_ref, mod_ref, g_ref, w_ref, ws_ref, o_ref, os_ref, h_ref, *, lat_len, ctx_row):
    b, i, j = pl.program_id(0), pl.program_id(1), pl.program_id(2)
    tm, d = h_ref.shape

    @pl.when(j == 0)
    def _():
        x = x_ref[0]
        ms = jnp.mean(x * x, axis=-1, keepdims=True)
        y = x * lax.rsqrt(ms + EPS) * g_ref[...]
        row = i * tm + lax.broadcasted_iota(jnp.int32, (tm, 1), 0)
        is_ctx = row >= lat_len
        mlat = mod_ref[pl.ds(b, 1), :]
        mctx = mod_ref[pl.ds(ctx_row, 1), :]
        shift = jnp.where(is_ctx, mctx[:, :d], mlat[:, :d])
        scale = jnp.where(is_ctx, mctx[:, d:2 * d], mlat[:, d:2 * d])
        h = (y * (1.0 + scale) + shift).astype(BF16)
        h_ref[...] = h
        os_ref[0] = _dot(h, ws_ref[...])

    o_ref[0] = _dot(h_ref[...], w_ref[...]).astype(BF16)


def _row_tile(ta, target):
    best = BF16_ROWS
    for t in range(BF16_ROWS, target + 1, BF16_ROWS):
        if ta % t == 0:
            best = t
    return best


def _inproj(xall, mod_l, g, w_big, w_small, ctx_len):
    nb, ta, d = xall.shape
    tm = _row_tile(ta, 1056)
    tn = 3456
    kern = functools.partial(_inproj_kernel, lat_len=ta - ctx_len, ctx_row=nb)
    return pl.pallas_call(
        kern,
        grid=(nb, ta // tm, N_BIG // tn),
        in_specs=[pl.BlockSpec((1, tm, d), lambda b, i, j: (b, i, 0)),
                  pl.BlockSpec(mod_l.shape, lambda b, i, j: (0, 0)),
                  pl.BlockSpec((1, d), lambda b, i, j: (0, 0)),
                  pl.BlockSpec((d, tn), lambda b, i, j: (0, j)),
                  pl.BlockSpec((d, LANES), lambda b, i, j: (0, 0))],
        out_specs=[pl.BlockSpec((1, tm, tn), lambda b, i, j: (b, i, j)),
                   pl.BlockSpec((1, tm, LANES), lambda b, i, j: (b, i, 0))],
        out_shape=[jax.ShapeDtypeStruct((nb, ta, N_BIG), BF16),
                   jax.ShapeDtypeStruct((nb, ta, LANES), F32)],
        scratch_shapes=[pltpu.VMEM((tm, d), BF16)],
        compiler_params=_cparams(("arbitrary", "arbitrary", "arbitrary")),
        name="inproj",
    )(xall, mod_l, g, w_big, w_small)


def _scan_batch(nb):
    return 4 if nb % 4 == 0 else (2 if nb % 2 == 0 else 1)


def _fwd_chunk(i, ncc, nch):
    return jnp.where(i < ncc, nch - ncc + i, i - ncc)


def _bwd_chunk(i, ncc, nch):
    del ncc
    return nch - 1 - i


def _full_spec(a):
    nd = a.ndim
    return pl.BlockSpec(a.shape, lambda b, i: (0,) * nd)


def _conv_kernel(xs_ref, xsp_ref, xsn_ref, xm_ref, xmp_ref, xmn_ref, cws_ref, cbs_ref, cwm_ref, cbm_ref,
                 us_ref, um_ref, *, lat_len, ta):
    tm = xs_ref.shape[1]
    row0 = pl.program_id(1) * tm
    has_prev = jnp.logical_and(row0 != 0, row0 != lat_len)
    has_next = jnp.logical_and(row0 + tm != lat_len, row0 + tm != ta)
    rid = lax.broadcasted_iota(jnp.int32, (tm, LANES), 0)
    for x_ref, xp_ref, xn_ref, cw_ref, cb_ref, o_ref in ((xs_ref, xsp_ref, xsn_ref, cws_ref, cbs_ref, us_ref),
                                                         (xm_ref, xmp_ref, xmn_ref, cwm_ref, cbm_ref, um_ref)):
        for j in range(x_ref.shape[2] // LANES):
            sl = slice(j * LANES, (j + 1) * LANES)
            x = x_ref[0, :, sl].astype(F32)
            prow = jnp.where(has_prev, xp_ref[0, BF16_ROWS - 1:BF16_ROWS, sl].astype(F32), 0.0)
            nrow = jnp.where(has_next, xn_ref[0, 0:1, sl].astype(F32), 0.0)
            xprev = jnp.where(rid == 0, prow, pltpu.roll(x, 1, 0))
            xnext = jnp.where(rid == tm - 1, nrow, pltpu.roll(x, tm - 1, 0))
            u = xprev * cw_ref[0:1, sl] + x * cw_ref[1:2, sl] + xnext * cw_ref[2:3, sl] + cb_ref[:, sl]
            o_ref[0, :, sl] = _silu(u).astype(BF16)


def _conv(p_big, cws, cbs, cwm, cbm, ctx_len):
    nb, ta, _ = p_big.shape
    tm = _row_tile(math.gcd(ta - ctx_len, ctx_len), 256)
    per = tm // BF16_ROWS
    nrow16 = ta // BF16_ROWS
    ws, wm = cws.shape[1], cwm.shape[1]

    def specs(width, col_block):
        return [pl.BlockSpec((1, tm, width), lambda b, i: (b, i, col_block)),
                pl.BlockSpec((1, BF16_ROWS, width), lambda b, i: (b, jnp.maximum(i * per - 1, 0), col_block)),
                pl.BlockSpec((1, BF16_ROWS, width),
                             lambda b, i: (b, jnp.minimum((i + 1) * per, nrow16 - 1), col_block))]

    consts = [cws, cbs, cwm, cbm]
    return pl.pallas_call(
        functools.partial(_conv_kernel, lat_len=ta - ctx_len, ta=ta),
        grid=(nb, ta // tm),
        in_specs=specs(ws, OFF_XBC // ws) + specs(wm, OFF_MQK // wm) + [_full_spec(a) for a in consts],
        out_specs=[pl.BlockSpec((1, tm, ws), lambda b, i: (b, i, 0)),
                   pl.BlockSpec((1, tm, wm), lambda b, i: (b, i, 0))],
        out_shape=[jax.ShapeDtypeStruct((nb, ta, ws), BF16), jax.ShapeDtypeStruct((nb, ta, wm), BF16)],
        compiler_params=_cparams(("arbitrary", "arbitrary")),
        name="conv_silu",
    )(p_big, p_big, p_big, p_big, p_big, p_big, *consts)


def _ssd_decay_sums(d, ps, dtb_ref, a_ref, tri3_ref, trit3_ref):
    dt = _softplus(ps + dtb_ref[...])
    a = dt * a_ref[...]
    acs = _dot_exact_lhs01(tri3_ref[d], a)
    acs_t = _dot_exact_rhs01(a.T, trit3_ref[d])
    return dict(dt=dt, acs=acs, acs_t=acs_t)


def _ssd_expansions(d, sd, e_ref):
    L = CHUNK
    last = L - 1 if d == 0 else 0
    dend = jnp.exp(sd["acs"][last:last + 1, :] - sd["acs"])
    e01 = e_ref[d]
    eacs_e = _expand_rhs01(jnp.exp(sd["acs"]), e01)
    sd.update(dt_e=_expand_rhs01(sd["dt"], e01), dtend_e=_expand_rhs01(sd["dt"] * dend, e01),
              eacs_e=eacs_e, cd_e=eacs_e[last:last + 1, :])


def _ssd_kernel(uf, ub, psf, psb, dtb, aneg, dsk, tri, tri3, trit3, e01, yf, yb, s_ref):
    @pl.when(pl.program_id(1) == 0)
    def _():
        s_ref[...] = jnp.zeros_like(s_ref)

    L = CHUNK
    hp = SSD_HEADS * SSD_HEAD_DIM
    gn = SSD_GROUPS * SSD_STATE
    gw = hp // SSD_GROUPS
    hpg = SSD_HEADS // SSD_GROUPS
    u_refs, y_refs, ps_refs = (uf, ub), (yf, yb), (psf, psb)
    bds = [(b, d) for b in range(uf.shape[0]) for d in range(2)]
    dgs = [(b, d, g) for b, d in bds for g in range(SSD_GROUPS)]

    cg, bg, sg, yo = {}, {}, {}, {}
    for b, d, g in dgs:
        bg[b, d, g] = u_refs[d][b, :, hp + g * SSD_STATE:hp + (g + 1) * SSD_STATE]
        cg[b, d, g] = u_refs[d][b, :, hp + gn + g * SSD_STATE:hp + gn + (g + 1) * SSD_STATE]
        sg[b, d, g] = _dot_nt(cg[b, d, g], bg[b, d, g])
        yo[b, d, g] = _dot(cg[b, d, g], s_ref[b, d, g].astype(BF16))
    st = {(b, d): _ssd_decay_sums(d, ps_refs[d][b], dtb, aneg, tri3, trit3) for b, d in bds}
    for b, d in bds:
        _ssd_expansions(d, st[b, d], e01)
    xs = {(b, d): u_refs[d][b, :, :hp].astype(F32) for b, d in bds}
    masks = [tri[d] > 0 for d in range(2)]
    left = lax.broadcasted_iota(jnp.int32, (L, LANES), 1) < SSD_HEAD_DIM

    pairs, snew = {}, {}
    for b, d, g in dgs:
        sd = st[b, d]
        xd = xs[b, d][:, g * gw:(g + 1) * gw] * sd["dt_e"][:, g * gw:(g + 1) * gw]
        for k in range(hpg // 2):
            ms = []
            for h in (g * hpg + 2 * k, g * hpg + 2 * k + 1):
                j = SSD_HEADS * d + h
                seg = sd["acs"][:, j:j + 1] - sd["acs_t"][j:j + 1, :]
                ms.append((sg[b, d, g] * jnp.exp(jnp.where(masks[d], seg, -jnp.inf))).astype(BF16))
            xp = xd[:, 2 * k * SSD_HEAD_DIM:(2 * k + 2) * SSD_HEAD_DIM]
            rhs = jnp.concatenate([jnp.where(left, xp, 0.0), jnp.where(left, 0.0, xp)], axis=0).astype(BF16)
            pairs[b, d, g, k] = _dot(jnp.concatenate(ms, axis=1), rhs)
        xde = (xs[b, d][:, g * gw:(g + 1) * gw] * sd["dtend_e"][:, g * gw:(g + 1) * gw]).astype(BF16)
        snew[b, d, g] = _dot_tn(bg[b, d, g], xde)

    for b, d in bds:
        ys = []
        for g in range(SSD_GROUPS):
            sl = slice(g * gw, (g + 1) * gw)
            ys.append(jnp.concatenate([pairs[b, d, g, k] for k in range(hpg // 2)], axis=1)
                      + yo[b, d, g] * st[b, d]["eacs_e"][:, sl])
            s_ref[b, d, g] = s_ref[b, d, g] * st[b, d]["cd_e"][:, sl] + snew[b, d, g]
        y = jnp.concatenate(ys, axis=1)
        if d == 0:
            y = y + dsk[...] * xs[b, d]
        y_refs[d][0, b] = y.astype(BF16)


def _ssd(u_ssd, p_small, dtb, aneg, dsk, tri, tri3, trit3, e01, ctx_len):
    nb, ta, width = u_ssd.shape
    ncc, nch = ctx_len // CHUNK, ta // CHUNK
    hp = SSD_HEADS * SSD_HEAD_DIM
    bb = _scan_batch(nb)
    xspecs = [pl.BlockSpec((bb, CHUNK, width), lambda b, i: (b, _fwd_chunk(i, ncc, nch), 0)),
              pl.BlockSpec((bb, CHUNK, width), lambda b, i: (b, _bwd_chunk(i, ncc, nch), 0))]
    ps_f = pl.BlockSpec((bb, CHUNK, LANES), lambda b, i: (b, _fwd_chunk(i, ncc, nch), 0))
    ps_b = pl.BlockSpec((bb, CHUNK, LANES), lambda b, i: (b, _bwd_chunk(i, ncc, nch), 0))
    consts = [dtb, aneg, dsk, tri, tri3, trit3, e01]
    yf, yb = pl.pallas_call(
        _ssd_kernel,
        grid=(nb // bb, nch),
        in_specs=xspecs + [ps_f, ps_b] + [_full_spec(a) for a in consts],
        out_specs=[pl.BlockSpec((1, bb, CHUNK, hp), lambda b, i: (0, b, _fwd_chunk(i, ncc, nch), 0)),
                   pl.BlockSpec((1, bb, CHUNK, hp), lambda b, i: (0, b, _bwd_chunk(i, ncc, nch), 0))],
        out_shape=[jax.ShapeDtypeStruct((1, nb, ta, hp), BF16)] * 2,
        scratch_shapes=[pltpu.VMEM((bb, 2, SSD_GROUPS, SSD_STATE, hp // SSD_GROUPS), F32)],
        compiler_params=_cparams(("arbitrary", "arbitrary")),
        name="ssd_scan",
    )(u_ssd, u_ssd, p_small, p_small, *consts)
    return yf[0], yb[0]


def _ml_stats(bds, ps_refs, ib_ref, fb_ref, trit3_ref, m_ref):
    L = CHUNK
    lane = lax.broadcasted_iota(jnp.int32, (8, L), 1)
    st = {}
    for b, d in bds:
        ps = ps_refs[d][b]
        li_t = (ps + ib_ref[...]).T
        lf_t = (-_softplus(-(ps + fb_ref[...]))).T
        bc = _dot_exact_rhs01(lf_t, trit3_ref[d])[SM_FG:SM_FG + 8, :]
        st[b, d] = dict(bc=bc, lir=li_t[SM_IG:SM_IG + 8, :])
    for b, d in bds:
        sd = st[b, d]
        last = L - 1 if d == 0 else 0
        m_prev = m_ref[b, d]
        b_last = jnp.broadcast_to(sd["bc"][:, last:last + 1], (8, L))
        gl = b_last - sd["bc"] + sd["lir"]
        m_loc = jnp.broadcast_to(jnp.max(gl, axis=-1, keepdims=True), (8, L))
        m_new = jnp.maximum(b_last + m_prev, m_loc)
        sd.update(w=jnp.exp(gl - m_loc), a_s=jnp.exp(b_last + m_prev - m_new), s_s=jnp.exp(m_loc - m_new),
                  m_new=m_new, rowb=sd["bc"] - sd["lir"], g=sd["lir"] - sd["bc"], e=sd["bc"] + m_prev)
    sh = 1
    while sh < L:
        for b, d in bds:
            g = st[b, d]["g"]
            if d == 0:
                st[b, d]["g"] = jnp.maximum(g, jnp.where(lane >= sh, pltpu.roll(g, sh, 1), -jnp.inf))
            else:
                st[b, d]["g"] = jnp.maximum(g, jnp.where(lane < L - sh, pltpu.roll(g, L - sh, 1), -jnp.inf))
        sh *= 2
    for b, d in bds:
        sd = st[b, d]
        m_t = jnp.maximum(sd["e"], sd["bc"] + sd["g"])
        sd["cols"] = jnp.concatenate([sd["bc"] - m_t, jnp.exp(sd["e"] - m_t), jnp.exp(-m_t),
                                      jnp.zeros((L - 24, L), F32)], axis=0).T
    return st


def _ml_kernel(uf, ub, vf, vb, psf, psb, ib, fb, tri, trit3, hf, hb, c_ref, n_ref, m_ref):
    @pl.when(pl.program_id(1) == 0)
    def _():
        c_ref[...] = jnp.zeros_like(c_ref)
        n_ref[...] = jnp.zeros_like(n_ref)
        m_ref[...] = jnp.zeros_like(m_ref)

    L = CHUNK
    hq = ML_HEADS * ML_QK_DIM
    rep = ML_V_DIM // LANES
    ones = jnp.ones((L, LANES), BF16)
    u_refs, v_refs, h_refs, ps_refs = (uf, ub), (vf, vb), (hf, hb), (psf, psb)
    masks = [tri[d] > 0 for d in range(2)]
    bds = [(b, d) for b in range(uf.shape[0]) for d in range(2)]
    units = [(b, d, h) for b, d in bds for h in range(ML_HEADS)]

    st1 = {}
    for b, d, h in units:
        q = u_refs[d][b, :, h * ML_QK_DIM:(h + 1) * ML_QK_DIM]
        k = u_refs[d][b, :, hq + h * ML_QK_DIM:hq + (h + 1) * ML_QK_DIM].astype(F32) * (ML_QK_DIM ** -0.5)
        st1[b, d, h] = dict(k=k, s=_dot_nt(q, k.astype(BF16)), qc=_dot(q, c_ref[b, d, h].astype(BF16)),
                            qn=_dot(q, n_ref[b, d, h].astype(BF16)))
    stats = _ml_stats(bds, ps_refs, ib, fb, trit3, m_ref)
    for b, d, h in units:
        r = ML_HEADS * d + h
        v = v_refs[d][b, :, h * ML_V_DIM:(h + 1) * ML_V_DIM]
        kwt = (st1[b, d, h]["k"].T * stats[b, d]["w"][r:r + 1, :]).astype(BF16)
        st1[b, d, h].update(v=v, c_loc=_dot(kwt, v), n_loc=_dot(kwt, ones))
    st2 = {}
    for b, d, h in units:
        r = ML_HEADS * d + h
        cola = stats[b, d]["cols"][:, r:r + 1]
        x = jnp.exp(jnp.where(masks[d], cola - stats[b, d]["rowb"][r:r + 1, :], -jnp.inf))
        wts = (x * st1[b, d, h]["s"]).astype(BF16)
        st2[b, d, h] = (_dot(wts, st1[b, d, h]["v"]), _dot(wts, ones))
    for b, d in bds:
        outs = []
        for h in range(ML_HEADS):
            r = ML_HEADS * d + h
            s1 = st1[b, d, h]
            sc = stats[b, d]["cols"][:, 8 + r:9 + r]
            emt = stats[b, d]["cols"][:, 16 + r:17 + r]
            wv, wo = st2[b, d, h]
            num = wv + s1["qc"] * sc
            den = wo + s1["qn"] * sc
            rden = 1.0 / jnp.maximum(jnp.abs(den), emt)
            outs.append(num * jnp.concatenate([rden] * rep, axis=1))
            a_r = stats[b, d]["a_s"][r:r + 1, :]
            s_r = stats[b, d]["s_s"][r:r + 1, :]
            c_ref[b, d, h] = (jnp.concatenate([a_r] * rep, axis=1) * c_ref[b, d, h]
                              + jnp.concatenate([s_r] * rep, axis=1) * s1["c_loc"])
            n_ref[b, d, h] = a_r * n_ref[b, d, h] + s_r * s1["n_loc"]
        h_refs[d][0, b] = jnp.concatenate(outs, axis=1).astype(BF16)
        m_ref[b, d] = stats[b, d]["m_new"]


def _mlstm(u_ml, p_big, p_small, ib, fb, tri, trit3, ctx_len):
    nb, ta, wq = u_ml.shape
    ncc, nch = ctx_len // CHUNK, ta // CHUNK
    wv = ML_HEADS * ML_V_DIM
    bb = _scan_batch(nb)
    qspecs = [pl.BlockSpec((bb, CHUNK, wq), lambda b, i: (b, _fwd_chunk(i, ncc, nch), 0)),
              pl.BlockSpec((bb, CHUNK, wq), lambda b, i: (b, _bwd_chunk(i, ncc, nch), 0))]
    v_f = pl.BlockSpec((bb, CHUNK, wv), lambda b, i: (b, _fwd_chunk(i, ncc, nch), OFF_MV // wv))
    v_b = pl.BlockSpec((bb, CHUNK, wv), lambda b, i: (b, _bwd_chunk(i, ncc, nch), OFF_MV // wv))
    ps_f = pl.BlockSpec((bb, CHUNK, LANES), lambda b, i: (b, _fwd_chunk(i, ncc, nch), 0))
    ps_b = pl.BlockSpec((bb, CHUNK, LANES), lambda b, i: (b, _bwd_chunk(i, ncc, nch), 0))
    consts = [ib, fb, tri, trit3]
    hf, hb = pl.pallas_call(
        _ml_kernel,
        grid=(nb // bb, nch),
        in_specs=qspecs + [v_f, v_b, ps_f, ps_b] + [_full_spec(a) for a in consts],
        out_specs=[pl.BlockSpec((1, bb, CHUNK, wv), lambda b, i: (0, b, _fwd_chunk(i, ncc, nch), 0)),
                   pl.BlockSpec((1, bb, CHUNK, wv), lambda b, i: (0, b, _bwd_chunk(i, ncc, nch), 0))],
        out_shape=[jax.ShapeDtypeStruct((1, nb, ta, wv), BF16)] * 2,
        scratch_shapes=[pltpu.VMEM((bb, 2, ML_HEADS, ML_QK_DIM, ML_V_DIM), F32),
                        pltpu.VMEM((bb, 2, ML_HEADS, ML_QK_DIM, LANES), F32),
                        pltpu.VMEM((bb, 2, 8, CHUNK), F32)],
        compiler_params=_cparams(("arbitrary", "arbitrary")),
        name="mlstm_scan",
    )(u_ml, u_ml, p_big, p_big, p_small, p_small, *consts)
    return hf[0], hb[0]


def _qk_prep_kernel(q_ref, k_ref, cos_ref, sa_ref, sb_ref, gq_ref, gk_ref, gm_ref, qo_ref, ko_ref, *, qscale):
    gm = gm_ref[...]
    nheads = q_ref.shape[2] // LANES
    tm = q_ref.shape[1]
    rb = _row_tile(tm, 192)

    def prep(t, g, mult, cos, sa, sb):
        t = t.astype(F32)
        sq = t * t
        sh = sq.astype(BF16)
        sm = (sq - sh.astype(F32)).astype(BF16)
        ms = (_dot(sh, gm) + _dot(sm, gm)) * (1.0 / DIFF_HEAD_DIM)
        y = t * lax.rsqrt(ms + EPS) * g
        r = y * cos + pltpu.roll(y, LANES - 16, 1) * sa + pltpu.roll(y, 16, 1) * sb
        return (r * mult).astype(BF16)

    for r in range(tm // rb):
        rows = slice(r * rb, (r + 1) * rb)
        tabs = (cos_ref[rows, :], sa_ref[rows, :], sb_ref[rows, :])
        for h in range(nheads):
            sl = slice(h * LANES, (h + 1) * LANES)
            qo_ref[0, rows, sl] = prep(q_ref[0, rows, sl], gq_ref[...], qscale, *tabs)
            ko_ref[0, rows, sl] = prep(k_ref[0, rows, sl], gk_ref[...], 1.0, *tabs)


def _qk_prep(p_big, cos, sa, sb, gq, gk, gm):
    nb, ta, _ = p_big.shape
    w = DIFF_HEADS * 2 * DIFF_HEAD_DIM
    tm = _row_tile(ta, 1056)
    qscale = DIFF_HEAD_DIM ** -0.5 * math.log2(math.e)
    row = lambda b, i: (i, 0)
    return pl.pallas_call(
        functools.partial(_qk_prep_kernel, qscale=qscale),
        grid=(nb, ta // tm),
        in_specs=[pl.BlockSpec((1, tm, w), lambda b, i: (b, i, OFF_Q // w)),
                  pl.BlockSpec((1, tm, w), lambda b, i: (b, i, OFF_K // w)),
                  pl.BlockSpec((tm, LANES), row), pl.BlockSpec((tm, LANES), row), pl.BlockSpec((tm, LANES), row),
                  _full_spec(gq), _full_spec(gk), _full_spec(gm)],
        out_specs=[pl.BlockSpec((1, tm, w), lambda b, i: (b, i, 0))] * 2,
        out_shape=[jax.ShapeDtypeStruct((nb, ta, w), BF16)] * 2,
        compiler_params=_cparams(("arbitrary", "arbitrary")),
        name="qk_prep",
    )(p_big, p_big, cos, sa, sb, gq, gk, gm)


def _attn_kernel(q_ref, k_ref, v_ref, z_ref, lam_ref, gq_ref, gk_ref, sg_ref, *rest, tk, lam_init):
    o_ref, e_ref, l_ref, r_ref, qq_ref = rest[-5:]
    nkeys = k_ref.shape[1]
    dh = DIFF_HEAD_DIM

    lp = lam_ref[...]
    lam = (jnp.exp(jnp.sum(lp[0:1] * lp[1:2], axis=-1, keepdims=True))
           - jnp.exp(jnp.sum(lp[2:3] * lp[3:4], axis=-1, keepdims=True)) + lam_init)
    shift = (jnp.max(jnp.abs(gq_ref[...]), axis=-1, keepdims=True)
             * jnp.max(jnp.abs(gk_ref[...]), axis=-1, keepdims=True)
             * (dh * dh ** -0.5 * math.log2(math.e)))

    tq = q_ref.shape[1]
    tsub = e_ref.shape[2]
    nsub = tq // tsub
    nchunk = nkeys // tk
    lane = lax.broadcasted_iota(jnp.int32, (tsub, LANES), 1)

    def scores_chunk(t, j):
        par = t % 2
        e = jnp.exp2(_dot_nt(qq_ref[t], k_ref[0, j * tk:(j + 1) * tk, :]) - shift)
        part = e[:, :LANES]
        for u in range(1, tk // LANES):
            part = part + e[:, u * LANES:(u + 1) * LANES]
        for c in range(2):
            pc = part[c * tsub:(c + 1) * tsub]
            l_ref[par, c] = pc if j == 0 else l_ref[par, c] + pc
            e_ref[par, c, :, j * tk:(j + 1) * tk] = e[c * tsub:(c + 1) * tsub].astype(BF16)

    def normalisers(t):
        par = t % 2
        l0 = jnp.sum(l_ref[par, 0], axis=-1, keepdims=True)
        l1 = jnp.sum(l_ref[par, 1], axis=-1, keepdims=True)
        r_ref[par] = jnp.broadcast_to(lam * l0 / l1, (tsub, LANES))
        return l0

    def combine_chunk(t, j):
        par = t % 2
        for u in range(tk // LANES):
            sl = slice(j * tk + u * LANES, j * tk + (u + 1) * LANES)
            e_ref[par, 0, :, sl] = (e_ref[par, 0, :, sl].astype(F32)
                                    - e_ref[par, 1, :, sl].astype(F32) * r_ref[par]).astype(BF16)

    def read_out(t, l0):
        par = t % 2
        rows = slice(t * tsub, (t + 1) * tsub)
        o = _dot(e_ref[par, 0], v_ref[0]) / l0
        ms = jnp.mean(o * o, axis=-1, keepdims=True)
        o = o * lax.rsqrt(ms + EPS) * sg_ref[...] * (1.0 - lam_init)
        o_ref[0, rows, :] = (o * _silu(z_ref[0, rows, :].astype(F32))).astype(BF16)

    for t in range(nsub):
        q = q_ref[0, t * tsub:(t + 1) * tsub, :]
        zero = jnp.zeros_like(q)
        qq_ref[t] = jnp.concatenate([jnp.where(lane < dh, q, zero), jnp.where(lane < dh, zero, q)], axis=0)

    for j in range(nchunk):
        scores_chunk(0, j)
    for t in range(1, nsub):
        l0 = normalisers(t - 1)
        for j in range(nchunk):
            scores_chunk(t, j)
            combine_chunk(t - 1, j)
        read_out(t - 1, l0)
    l0 = normalisers(nsub - 1)
    for j in range(nchunk):
        combine_chunk(nsub - 1, j)
    read_out(nsub - 1, l0)


def _attention(qn, kn, p_big, lam_p, gq, gk, sg, ctx_len, lam_init):
    nb, ta, w = qn.shape
    seq = ta - ctx_len
    tk = 256
    tsub = 256
    tq = 1024 if seq % 1024 == 0 else 512
    assert seq % tq == 0 and seq % ctx_len == 0 and ta % tk == 0 and ctx_len % tk == 0 and ctx_len == tsub
    kern = functools.partial(_attn_kernel, tk=tk, lam_init=lam_init)
    full = lambda a: pl.BlockSpec(a.shape, lambda b, h, i: (0,) * a.ndim)
    consts = [lam_p, gq, gk, sg]

    def call(tq_, nkeys, row_block, key_block, nq, prev):
        in_specs = [pl.BlockSpec((1, tq_, LANES), lambda b, h, i: (b, row_block + i, h)),
                    pl.BlockSpec((1, nkeys, LANES), lambda b, h, i: (b, key_block, h)),
                    pl.BlockSpec((1, nkeys, LANES), lambda b, h, i: (b, key_block, OFF_V // LANES + h)),
                    pl.BlockSpec((1, tq_, LANES), lambda b, h, i: (b, row_block + i, OFF_ZD // LANES + h))]
        in_specs += [full(a) for a in consts]
        args = [qn, kn, p_big, p_big] + consts
        aliases = {}
        if prev is not None:
            in_specs.append(pl.BlockSpec(memory_space=pl.ANY))
            aliases = {len(args): 0}
            args.append(prev)
        return pl.pallas_call(
            kern,
            grid=(nb, DIFF_HEADS, nq),
            in_specs=in_specs,
            out_specs=pl.BlockSpec((1, tq_, LANES), lambda b, h, i: (b, row_block + i, h)),
            out_shape=jax.ShapeDtypeStruct((nb, ta, w), BF16),
            scratch_shapes=[pltpu.VMEM((2, 2, tsub, nkeys), BF16), pltpu.VMEM((2, 2, tsub, LANES), F32),
                            pltpu.VMEM((2, tsub, LANES), F32), pltpu.VMEM((tq_ // tsub, 2 * tsub, LANES), BF16)],
            input_output_aliases=aliases,
            compiler_params=_cparams(("arbitrary", "arbitrary", "arbitrary")),
            name="diff_attention" if prev is None else "diff_attention_ctx",
        )(*args)

    yd = call(tq, ta, 0, 0, seq // tq, None)
    return call(ctx_len, ctx_len, seq // ctx_len, seq // ctx_len, 1, yd)


def _merge_kernel(x_ref, mod_ref, yf_ref, yb_ref, zs_ref, yd_ref, hf_ref, hb_ref, mo_ref, mz_ref, gt_ref,
                  sng_ref, mng_ref, wb_ref, wo_ref, o_ref, *, lat_len, ctx_row):
    b, i = pl.program_id(0), pl.program_id(1)
    tm, d = x_ref.shape[1], x_ref.shape[2]

    pd = _dot(yd_ref[0], wb_ref[1])
    ya = (yf_ref[0].astype(F32) + yb_ref[0].astype(F32)) * _silu(zs_ref[0].astype(F32))
    ya = ya * lax.rsqrt(jnp.mean(ya * ya, axis=-1, keepdims=True) + EPS) * sng_ref[...]
    pa = _dot(ya.astype(BF16), wb_ref[0])

    hm = (hf_ref[0].astype(F32) + hb_ref[0].astype(F32)) * _sigmoid(mo_ref[0].astype(F32))
    parts = []
    for h in range(ML_HEADS):
        t = hm[:, h * ML_V_DIM:(h + 1) * ML_V_DIM]
        parts.append(t * lax.rsqrt(jnp.mean(t * t, axis=-1, keepdims=True) + EPS))
    yc = jnp.concatenate(parts, axis=1) * mng_ref[...] * _silu(mz_ref[0].astype(F32))

    g = _sigmoid(gt_ref[0].astype(F32))
    mixed = (g[:, :d] * pa + g[:, d:2 * d] * pd + g[:, 2 * d:] * _dot(yc.astype(BF16), wb_ref[2]))
    out = _dot(mixed.astype(BF16), wo_ref[...])

    row = i * tm + lax.broadcasted_iota(jnp.int32, (tm, 1), 0)
    is_ctx = row >= lat_len
    gate = jnp.where(is_ctx, mod_ref[pl.ds(ctx_row, 1), 2 * d:], mod_ref[pl.ds(b, 1), 2 * d:])
    o_ref[0] = x_ref[0] + gate * out


def _merge(xall, mod_l, yf, yb, p_big, yd, hf, hb, sng, mng, wb, wo, ctx_len, out_rows):
    nb, ta, d = xall.shape
    tm = _row_tile(out_rows, 352)
    kern = functools.partial(_merge_kernel, lat_len=ta - ctx_len, ctx_row=nb)
    blk = lambda col: pl.BlockSpec((1, tm, d), lambda b, i: (b, i, col))
    full = lambda a: pl.BlockSpec(a.shape, lambda b, i: (0,) * a.ndim)
    return pl.pallas_call(
        kern,
        grid=(nb, out_rows // tm),
        in_specs=[blk(0), full(mod_l), blk(0), blk(0), blk(OFF_ZS // d), blk(0), blk(0), blk(0),
                  blk(OFF_MO // d), blk(OFF_MZ // d),
                  pl.BlockSpec((1, tm, 3 * d), lambda b, i: (b, i, OFF_GATES // (3 * d))),
                  full(sng), full(mng), full(wb), full(wo)],
        out_specs=blk(0),
        out_shape=jax.ShapeDtypeStruct((nb, out_rows, d), F32),
        compiler_params=_cparams(("arbitrary", "arbitrary")),
        name="merge",
    )(xall, mod_l, yf, yb, p_big, yd, hf, hb, p_big, p_big, p_big, sng, mng, wb, wo)


def _rope_tables(ctx_len, seq):
    rows = seq // GRID_W
    row = jnp.repeat(jnp.arange(rows, dtype=F32), GRID_W)
    col = jnp.tile(jnp.arange(GRID_W, dtype=F32), rows)
    half = DIFF_HEAD_DIM // 2
    inv_freq = ROPE_BASE ** (-jnp.arange(0, half, 2, dtype=F32) / half)
    ang_r = row[:, None] * inv_freq
    ang_c = col[:, None] * inv_freq
    cos = jnp.concatenate([jnp.cos(ang_r), jnp.cos(ang_r), jnp.cos(ang_c), jnp.cos(ang_c)], axis=-1)
    sin = jnp.concatenate([jnp.sin(ang_r), jnp.sin(ang_r), jnp.sin(ang_c), jnp.sin(ang_c)], axis=-1)
    cos = jnp.concatenate([cos, jnp.ones((ctx_len, DIFF_HEAD_DIM), F32)], axis=0)
    sin = jnp.concatenate([sin, jnp.zeros((ctx_len, DIFF_HEAD_DIM), F32)], axis=0)
    cos = jnp.tile(cos, (1, LANES // DIFF_HEAD_DIM))
    sin = jnp.tile(sin, (1, LANES // DIFF_HEAD_DIM))
    first = (jnp.arange(LANES) % (DIFF_HEAD_DIM // 2)) < DIFF_HEAD_DIM // 4
    return cos, jnp.where(first, -sin, 0.0), jnp.where(first, 0.0, sin)


def _scan_constants():
    idx = np.arange(CHUNK)
    tri_f = (idx[None, :] <= idx[:, None]).astype(np.float32)
    tri = np.stack([tri_f, tri_f.T])
    trit = np.stack([tri_f.T, tri_f])
    e01 = np.zeros((2, LANES, SSD_HEADS * SSD_HEAD_DIM), np.float32)
    for d in range(2):
        for h in range(SSD_HEADS):
            e01[d, SSD_HEADS * d + h, h * SSD_HEAD_DIM:(h + 1) * SSD_HEAD_DIM] = 1.0
    half = np.arange(LANES) // DIFF_HEAD_DIM
    gm = (half[:, None] == half[None, :]).astype(np.float32)
    tri3 = np.concatenate([tri] * 3, axis=2)
    trit3 = np.concatenate([trit] * 3, axis=1)
    e01x2 = np.concatenate([e01] * 2, axis=1)
    return (jnp.asarray(tri, BF16), jnp.asarray(tri3, BF16), jnp.asarray(trit3, BF16), jnp.asarray(e01x2, BF16),
            jnp.asarray(gm, BF16))


def _pad_row(v, offset):
    v = v.reshape(1, -1).astype(F32)
    return jnp.pad(v, ((0, 0), (offset, LANES - offset - v.shape[1])))


def kernel(x, c, ctx, c_ctx, w_mod, b_mod, norm_g, w_in, ssd_conv_w, ssd_conv_b, ssd_a_log, ssd_dt_bias, ssd_d,
           ssd_norm_g, diff_qn_g, diff_kn_g, diff_lambda, diff_subln_g, ml_conv_w, ml_conv_b, ml_i_bias,
           ml_f_bias, ml_norm_g, w_branch, w_out):
    nb, seq, d = x.shape
    ctx_len = ctx.shape[1]
    depth = w_mod.shape[0]
    assert d == 1024 and ctx_len % CHUNK == 0 and seq % CHUNK == 0 and seq % GRID_W == 0 and nb < 8

    sizes = (1536, 1024, 32, 1024, 1024, 1024, 1024, 1024, 1024, 1024, 1024, 8, 8, 3072)
    offs = np.concatenate([[0], np.cumsum(sizes)])
    seg = lambda n: w_in[:, :, offs[n]:offs[n + 1]]
    w_big = jnp.concatenate([seg(1), seg(3), seg(4), seg(5), seg(6), seg(7), seg(8), seg(9), seg(10),
                             seg(13), seg(0)], axis=-1).astype(BF16)
    w_small = jnp.concatenate([seg(2), seg(11), seg(12)], axis=-1)
    w_small = jnp.pad(w_small, ((0, 0), (0, 0), (0, LANES - w_small.shape[-1]))).astype(BF16)

    cc = jnp.concatenate([c, c_ctx[None, :], jnp.zeros((8 - nb - 1, d), F32)], axis=0)
    mod = _modulation(cc, w_mod, b_mod)

    tri, tri3, trit3, e01, gm = _scan_constants()
    cos, sa, sb = _rope_tables(ctx_len, seq)
    wb = w_branch.astype(BF16)
    wo = w_out.astype(BF16)
    xall = jnp.concatenate([x, ctx], axis=1)
    ta = seq + ctx_len

    for l in range(depth):
        lam_init = 0.8 - 0.6 * math.exp(-0.3 * l)
        p_big, p_small = _inproj(xall, mod[l], norm_g[l][None, :], w_big[l], w_small[l], ctx_len)
        u_ssd, u_ml = _conv(p_big, ssd_conv_w[l], ssd_conv_b[l][None, :], ml_conv_w[l], ml_conv_b[l][None, :],
                            ctx_len)
        yf, yb = _ssd(u_ssd, p_small, _pad_row(ssd_dt_bias[l], 0), _pad_row(-jnp.exp(ssd_a_log[l]), 0),
                      jnp.repeat(ssd_d[l], SSD_HEAD_DIM)[None, :], tri, tri3, trit3, e01, ctx_len)
        qn, kn = _qk_prep(p_big, cos, sa, sb, jnp.tile(diff_qn_g[l], 2)[None, :],
                          jnp.tile(diff_kn_g[l], 2)[None, :], gm)
        yd = _attention(qn, kn, p_big, diff_lambda[l], diff_qn_g[l][None, :], diff_kn_g[l][None, :],
                        diff_subln_g[l][None, :], ctx_len, lam_init)
        hf, hb = _mlstm(u_ml, p_big, p_small, _pad_row(ml_i_bias[l], SM_IG), _pad_row(ml_f_bias[l], SM_FG),
                        tri, trit3, ctx_len)
        xall = _merge(xall, mod[l], yf, yb, p_big, yd, hf, hb, ssd_norm_g[l][None, :], ml_norm_g[l][None, :],
                      wb[l], wo[l], ctx_len, ta if l + 1 < depth else seq)
    return xall
```

```python
import functools
import math

import numpy as np
import jax
import jax.numpy as jnp
from jax import lax
from jax.experimental import pallas as pl
from jax.experimental.pallas import tpu as pltpu

F32 = jnp.float32
BF16 = jnp.bfloat16

GRID_W = 64
EPS = 1e-6
CONV_K = 3
SSD_HEADS = 16
SSD_HEAD_DIM = 64
SSD_GROUPS = 2
SSD_STATE = 128
DIFF_HEADS = 8
DIFF_HEAD_DIM = 64
ROPE_BASE = 10000.0
ML_HEADS = 4
ML_QK_DIM = 128
ML_V_DIM = 256
CHUNK = 128
LANES = 128
BF16_ROWS = 16

OFF_ZS, OFF_Q, OFF_K, OFF_V, OFF_ZD = 0, 1024, 2048, 3072, 4096
OFF_MQK, OFF_MV, OFF_MO, OFF_MZ, OFF_GATES, OFF_XBC = 5120, 6144, 7168, 8192, 9216, 12288
N_BIG = 13824
SM_IG, SM_FG = 32, 40

VMEM_LIMIT = 56 * 1024 * 1024


def _cparams(sem):
    return pltpu.CompilerParams(dimension_semantics=sem, vmem_limit_bytes=VMEM_LIMIT)


def _split3(v):
    h = v.astype(BF16)
    r = v - h.astype(F32)
    m = r.astype(BF16)
    l = (r - m.astype(F32)).astype(BF16)
    return h, m, l


def _dot(a, b):
    return jnp.dot(a, b, preferred_element_type=F32)


def _dot_nt(a, b):
    return lax.dot_general(a, b, (((1,), (1,)), ((), ())), preferred_element_type=F32)


def _dot_tn(a, b):
    return lax.dot_general(a, b, (((0,), (0,)), ((), ())), preferred_element_type=F32)


def _dot_exact_rhs01(v, m01x3):
    return _dot(jnp.concatenate(_split3(v), axis=1), m01x3)


def _expand_rhs01(v, m01x2):
    h = v.astype(BF16)
    m = (v - h.astype(F32)).astype(BF16)
    return _dot(jnp.concatenate([h, m], axis=1), m01x2)


def _dot_exact_lhs01(m01x3, v):
    return _dot(m01x3, jnp.concatenate(_split3(v), axis=0))


def _sigmoid(x):
    return 1.0 / (1.0 + jnp.exp2(x * (-math.log2(math.e))))


def _silu(x):
    return x * _sigmoid(x)


def _softplus(x):
    return jnp.maximum(x, 0.0) + jnp.log(1.0 + jnp.exp(-jnp.abs(x)))


def _mod_kernel(cc_ref, w_ref, b_ref, o_ref):
    a = _silu(cc_ref[...])
    w = w_ref[0]
    ah, am, _ = _split3(a)
    wh = w.astype(BF16)
    wm = (w - wh.astype(F32)).astype(BF16)
    o_ref[0] = _dot(ah, wh) + _dot(am, wh) + _dot(ah, wm) + b_ref[0]


def _modulation(cc, w_mod, b_mod):
    depth, d, d3 = w_mod.shape
    tn = 1024
    return pl.pallas_call(
        _mod_kernel,
        grid=(depth, d3 // tn),
        in_specs=[pl.BlockSpec((8, d), lambda l, j: (0, 0)),
                  pl.BlockSpec((1, d, tn), lambda l, j: (l, 0, j)),
                  pl.BlockSpec((1, 1, tn), lambda l, j: (l, 0, j))],
        out_specs=pl.BlockSpec((1, 8, tn), lambda l, j: (l, 0, j)),
        out_shape=jax.ShapeDtypeStruct((depth, 8, d3), F32),
        compiler_params=_cparams(("arbitrary", "arbitrary")),
        name="modulation",
    )(cc, w_mod, b_mod.reshape(depth, 1, d3))


def _inproj_kernel(x_ref, mod_ref, g_ref, w_ref, ws_ref, o_ref, os_ref, h_ref, *, lat_len, ctx_row):
    b, i, j = pl.program_id(0), pl.program_id(1), pl.program_id(2)
    tm, d = h_ref.shape

    @pl.when(j == 0)
    def _():
        x = x_ref[0]
        ms = jnp.mean(x * x, axis=-1, keepdims=True)
        y = x * lax.rsqrt(ms + EPS) * g_ref[...]
        row = i * tm + lax.broadcasted_iota(jnp.int32, (tm, 1), 0)
        is_ctx = row >= lat_len
        mlat = mod_ref[pl.ds(b, 1), :]
        mctx = mod_ref[pl.ds(ctx_row, 1), :]
        shift = jnp.where(is_ctx, mctx[:, :d], mlat[:, :d])
        scale = jnp.where(is_ctx, mctx[:, d:2 * d], mlat[:, d:2 * d])
        h = (y * (1.0 + scale) + shift).astype(BF16)
        h_ref[...] = h
        os_ref[0] = _dot(h, ws_ref[...])

    o_ref[0] = _dot(h_ref[...], w_ref[...]).astype(BF16)


def _row_tile(ta, target):
    best = BF16_ROWS
    for t in range(BF16_ROWS, target + 1, BF16_ROWS):
        if ta % t == 0:
            best = t
    return best


def _inproj(xall, mod_l, g, w_big, w_small, ctx_len):
    nb, ta, d = xall.shape
    tm = _row_tile(ta, 1056)
    tn = 3456
    kern = functools.partial(_inproj_kernel, lat_len=ta - ctx_len, ctx_row=nb)
    return pl.pallas_call(
        kern,
        grid=(nb, ta // tm, N_BIG // tn),
        in_specs=[pl.BlockSpec((1, tm, d), lambda b, i, j: (b, i, 0)),
                  pl.BlockSpec(mod_l.shape, lambda b, i, j: (0, 0)),
                  pl.BlockSpec((1, d), lambda b, i, j: (0, 0)),
                  pl.BlockSpec((d, tn), lambda b, i, j: (0, j)),
                  pl.BlockSpec((d, LANES), lambda b, i, j: (0, 0))],
        out_specs=[pl.BlockSpec((1, tm, tn), lambda b, i, j: (b, i, j)),
                   pl.BlockSpec((1, tm, LANES), lambda b, i, j: (b, i, 0))],
        out_shape=[jax.ShapeDtypeStruct((nb, ta, N_BIG), BF16),
                   jax.ShapeDtypeStruct((nb, ta, LANES), F32)],
        scratch_shapes=[pltpu.VMEM((tm, d), BF16)],
        compiler_params=_cparams(("arbitrary", "arbitrary", "arbitrary")),
        name="inproj",
    )(xall, mod_l, g, w_big, w_small)


def _scan_batch(nb):
    return 4 if nb % 4 == 0 else (2 if nb % 2 == 0 else 1)


def _fwd_chunk(i, ncc, nch):
    return jnp.where(i < ncc, nch - ncc + i, i - ncc)


def _bwd_chunk(i, ncc, nch):
    del ncc
    return nch - 1 - i


def _full_spec(a):
    nd = a.ndim
    return pl.BlockSpec(a.shape, lambda b, i: (0,) * nd)


def _conv_kernel(xs_ref, xsp_ref, xsn_ref, xm_ref, xmp_ref, xmn_ref, cws_ref, cbs_ref, cwm_ref, cbm_ref,
                 us_ref, um_ref, *, lat_len, ta):
    tm = xs_ref.shape[1]
    row0 = pl.program_id(1) * tm
    has_prev = jnp.logical_and(row0 != 0, row0 != lat_len)
    has_next = jnp.logical_and(row0 + tm != lat_len, row0 + tm != ta)
    ri = lax.broadcasted_iota(jnp.int32, (tm, tm), 0)
    ci = lax.broadcasted_iota(jnp.int32, (tm, tm), 1)
    down = (ri == ci + 1).astype(BF16)
    up = (ri + 1 == ci).astype(BF16)
    sub = lax.broadcasted_iota(jnp.int32, (8, LANES), 0)
    for x_ref, xp_ref, xn_ref, cw_ref, cb_ref, o_ref in ((xs_ref, xsp_ref, xsn_ref, cws_ref, cbs_ref, us_ref),
                                                         (xm_ref, xmp_ref, xmn_ref, cwm_ref, cbm_ref, um_ref)):
        xb = x_ref[0]
        xprev_all = _dot(down, xb)
        xnext_all = _dot(up, xb)
        for j in range(x_ref.shape[2] // LANES):
            sl = slice(j * LANES, (j + 1) * LANES)
            w0, w1, w2 = cw_ref[0:1, sl], cw_ref[1:2, sl], cw_ref[2:3, sl]
            u = xprev_all[:, sl] * w0 + xb[:, sl].astype(F32) * w1 + xnext_all[:, sl] * w2 + cb_ref[:, sl]
            prow = jnp.where(has_prev, xp_ref[0, BF16_ROWS - 1:BF16_ROWS, sl].astype(F32), 0.0) * w0
            nrow = jnp.where(has_next, xn_ref[0, 0:1, sl].astype(F32), 0.0) * w2
            u = jnp.concatenate([u[:8] + jnp.where(sub == 0, prow, 0.0), u[8:tm - 8],
                                 u[tm - 8:] + jnp.where(sub == 7, nrow, 0.0)], axis=0)
            o_ref[0, :, sl] = _silu(u).astype(BF16)


def _conv(p_big, cws, cbs, cwm, cbm, ctx_len):
    nb, ta, _ = p_big.shape
    tm = _row_tile(math.gcd(ta - ctx_len, ctx_len), 256)
    per = tm // BF16_ROWS
    nrow16 = ta // BF16_ROWS
    ws, wm = cws.shape[1], cwm.shape[1]

    def specs(width, col_block):
        return [pl.BlockSpec((1, tm, width), lambda b, i: (b, i, col_block)),
                pl.BlockSpec((1, BF16_ROWS, width), lambda b, i: (b, jnp.maximum(i * per - 1, 0), col_block)),
                pl.BlockSpec((1, BF16_ROWS, width),
                             lambda b, i: (b, jnp.minimum((i + 1) * per, nrow16 - 1), col_block))]

    consts = [cws, cbs, cwm, cbm]
    return pl.pallas_call(
        functools.partial(_conv_kernel, lat_len=ta - ctx_len, ta=ta),
        grid=(nb, ta // tm),
        in_specs=specs(ws, OFF_XBC // ws) + specs(wm, OFF_MQK // wm) + [_full_spec(a) for a in consts],
        out_specs=[pl.BlockSpec((1, tm, ws), lambda b, i: (b, i, 0)),
                   pl.BlockSpec((1, tm, wm), lambda b, i: (b, i, 0))],
        out_shape=[jax.ShapeDtypeStruct((nb, ta, ws), BF16), jax.ShapeDtypeStruct((nb, ta, wm), BF16)],
        compiler_params=_cparams(("arbitrary", "arbitrary")),
        name="conv_silu",
    )(p_big, p_big, p_big, p_big, p_big, p_big, *consts)


def _ssd_decay_sums(d, ps, dtb_ref, a_ref, tri3_ref, trit3_ref):
    dt = _softplus(ps + dtb_ref[...])
    a = dt * a_ref[...]
    acs = _dot_exact_lhs01(tri3_ref[d], a)
    acs_t = _dot_exact_rhs01(a.T, trit3_ref[d])
    return dict(dt=dt, acs=acs, acs_t=acs_t)


def _ssd_expansions(d, sd, e_ref):
    L = CHUNK
    last = L - 1 if d == 0 else 0
    dend = jnp.exp(sd["acs"][last:last + 1, :] - sd["acs"])
    e01 = e_ref[d]
    eacs_e = _expand_rhs01(jnp.exp(sd["acs"]), e01)
    sd.update(dt_e=_expand_rhs01(sd["dt"], e01), dtend_e=_expand_rhs01(sd["dt"] * dend, e01),
              eacs_e=eacs_e, cd_e=eacs_e[last:last + 1, :])


def _ssd_kernel(uf, ub, psf, psb, dtb, aneg, dsk, tri, tri3, trit3, e01, yf, yb, s_ref):
    @pl.when(pl.program_id(1) == 0)
    def _():
        s_ref[...] = jnp.zeros_like(s_ref)

    L = CHUNK
    hp = SSD_HEADS * SSD_HEAD_DIM
    gn = SSD_GROUPS * SSD_STATE
    gw = hp // SSD_GROUPS
    hpg = SSD_HEADS // SSD_GROUPS
    u_refs, y_refs, ps_refs = (uf, ub), (yf, yb), (psf, psb)
    bds = [(b, d) for b in range(uf.shape[0]) for d in range(2)]
    dgs = [(b, d, g) for b, d in bds for g in range(SSD_GROUPS)]

    cg, bg, sg, yo = {}, {}, {}, {}
    for b, d, g in dgs:
        bg[b, d, g] = u_refs[d][b, :, hp + g * SSD_STATE:hp + (g + 1) * SSD_STATE]
        cg[b, d, g] = u_refs[d][b, :, hp + gn + g * SSD_STATE:hp + gn + (g + 1) * SSD_STATE]
        sg[b, d, g] = _dot_nt(cg[b, d, g], bg[b, d, g])
        yo[b, d, g] = _dot(cg[b, d, g], s_ref[b, d, g].astype(BF16))
    st = {(b, d): _ssd_decay_sums(d, ps_refs[d][b], dtb, aneg, tri3, trit3) for b, d in bds}
    for b, d in bds:
        _ssd_expansions(d, st[b, d], e01)
    xs = {(b, d): u_refs[d][b, :, :hp].astype(F32) for b, d in bds}
    masks = [tri[d] > 0 for d in range(2)]
    left = lax.broadcasted_iota(jnp.int32, (L, LANES), 1) < SSD_HEAD_DIM

    pairs, snew = {}, {}
    for b, d, g in dgs:
        sd = st[b, d]
        xd = xs[b, d][:, g * gw:(g + 1) * gw] * sd["dt_e"][:, g * gw:(g + 1) * gw]
        for k in range(hpg // 2):
            ms = []
            for h in (g * hpg + 2 * k, g * hpg + 2 * k + 1):
                j = SSD_HEADS * d + h
                seg = sd["acs"][:, j:j + 1] - sd["acs_t"][j:j + 1, :]
                ms.append((sg[b, d, g] * jnp.exp(jnp.where(masks[d], seg, -jnp.inf))).astype(BF16))
            xp = xd[:, 2 * k * SSD_HEAD_DIM:(2 * k + 2) * SSD_HEAD_DIM]
            rhs = jnp.concatenate([jnp.where(left, xp, 0.0), jnp.where(left, 0.0, xp)], axis=0).astype(BF16)
            pairs[b, d, g, k] = _dot(jnp.concatenate(ms, axis=1), rhs)
        xde = (xs[b, d][:, g * gw:(g + 1) * gw] * sd["dtend_e"][:, g * gw:(g + 1) * gw]).astype(BF16)
        snew[b, d, g] = _dot_tn(bg[b, d, g], xde)

    for b, d in bds:
        ys = []
        for g in range(SSD_GROUPS):
            sl = slice(g * gw, (g + 1) * gw)
            ys.append(jnp.concatenate([pairs[b, d, g, k] for k in range(hpg // 2)], axis=1)
                      + yo[b, d, g] * st[b, d]["eacs_e"][:, sl])
            s_ref[b, d, g] = s_ref[b, d, g] * st[b, d]["cd_e"][:, sl] + snew[b, d, g]
        y = jnp.concatenate(ys, axis=1)
        if d == 0:
            y = y + dsk[...] * xs[b, d]
        y_refs[d][0, b] = y.astype(BF16)


def _ssd(u_ssd, p_small, dtb, aneg, dsk, tri, tri3, trit3, e01, ctx_len):
    nb, ta, width = u_ssd.shape
    ncc, nch = ctx_len // CHUNK, ta // CHUNK
    hp = SSD_HEADS * SSD_HEAD_DIM
    bb = _scan_batch(nb)
    xspecs = [pl.BlockSpec((bb, CHUNK, width), lambda b, i: (b, _fwd_chunk(i, ncc, nch), 0)),
              pl.BlockSpec((bb, CHUNK, width), lambda b, i: (b, _bwd_chunk(i, ncc, nch), 0))]
    ps_f = pl.BlockSpec((bb, CHUNK, LANES), lambda b, i: (b, _fwd_chunk(i, ncc, nch), 0))
    ps_b = pl.BlockSpec((bb, CHUNK, LANES), lambda b, i: (b, _bwd_chunk(i, ncc, nch), 0))
    consts = [dtb, aneg, dsk, tri, tri3, trit3, e01]
    yf, yb = pl.pallas_call(
        _ssd_kernel,
        grid=(nb // bb, nch),
        in_specs=xspecs + [ps_f, ps_b] + [_full_spec(a) for a in consts],
        out_specs=[pl.BlockSpec((1, bb, CHUNK, hp), lambda b, i: (0, b, _fwd_chunk(i, ncc, nch), 0)),
                   pl.BlockSpec((1, bb, CHUNK, hp), lambda b, i: (0, b, _bwd_chunk(i, ncc, nch), 0))],
        out_shape=[jax.ShapeDtypeStruct((1, nb, ta, hp), BF16)] * 2,
        scratch_shapes=[pltpu.VMEM((bb, 2, SSD_GROUPS, SSD_STATE, hp // SSD_GROUPS), F32)],
        compiler_params=_cparams(("arbitrary", "arbitrary")),
        name="ssd_scan",
    )(u_ssd, u_ssd, p_small, p_small, *consts)
    return yf[0], yb[0]


def _ml_stats(bds, ps_refs, ib_ref, fb_ref, trit3_ref, m_ref):
    L = CHUNK
    lane = lax.broadcasted_iota(jnp.int32, (8, L), 1)
    st = {}
    for b, d in bds:
        ps = ps_refs[d][b]
        li_t = (ps + ib_ref[...]).T
        lf_t = (-_softplus(-(ps + fb_ref[...]))).T
        bc = _dot_exact_rhs01(lf_t, trit3_ref[d])[SM_FG:SM_FG + 8, :]
        st[b, d] = dict(bc=bc, lir=li_t[SM_IG:SM_IG + 8, :])
    for b, d in bds:
        sd = st[b, d]
        last = L - 1 if d == 0 else 0
        m_prev = m_ref[b, d]
        b_last = jnp.broadcast_to(sd["bc"][:, last:last + 1], (8, L))
        gl = b_last - sd["bc"] + sd["lir"]
        m_loc = jnp.broadcast_to(jnp.max(gl, axis=-1, keepdims=True), (8, L))
        m_new = jnp.maximum(b_last + m_prev, m_loc)
        sd.update(w=jnp.exp(gl - m_loc), a_s=jnp.exp(b_last + m_prev - m_new), s_s=jnp.exp(m_loc - m_new),
                  m_new=m_new, rowb=sd["bc"] - sd["lir"], g=sd["lir"] - sd["bc"], e=sd["bc"] + m_prev)
    sh = 1
    while sh < L:
        for b, d in bds:
            g = st[b, d]["g"]
            if d == 0:
                st[b, d]["g"] = jnp.maximum(g, jnp.where(lane >= sh, pltpu.roll(g, sh, 1), -jnp.inf))
            else:
                st[b, d]["g"] = jnp.maximum(g, jnp.where(lane < L - sh, pltpu.roll(g, L - sh, 1), -jnp.inf))
        sh *= 2
    for b, d in bds:
        sd = st[b, d]
        m_t = jnp.maximum(sd["e"], sd["bc"] + sd["g"])
        sd["cols"] = jnp.concatenate([sd["bc"] - m_t, jnp.exp(sd["e"] - m_t), jnp.exp(-m_t),
                                      jnp.zeros((L - 24, L), F32)], axis=0).T
    return st


def _ml_kernel(uf, ub, vf, vb, psf, psb, ib, fb, tri, trit3, hf, hb, c_ref, n_ref, m_ref):
    @pl.when(pl.program_id(1) == 0)
    def _():
        c_ref[...] = jnp.zeros_like(c_ref)
        n_ref[...] = jnp.zeros_like(n_ref)
        m_ref[...] = jnp.zeros_like(m_ref)

    L = CHUNK
    hq = ML_HEADS * ML_QK_DIM
    rep = ML_V_DIM // LANES
    ones = jnp.ones((L, LANES), BF16)
    u_refs, v_refs, h_refs, ps_refs = (uf, ub), (vf, vb), (hf, hb), (psf, psb)
    masks = [tri[d] > 0 for d in range(2)]
    bds = [(b, d) for b in range(uf.shape[0]) for d in range(2)]
    units = [(b, d, h) for b, d in bds for h in range(ML_HEADS)]

    st1 = {}
    for b, d, h in units:
        q = u_refs[d][b, :, h * ML_QK_DIM:(h + 1) * ML_QK_DIM]
        k = u_refs[d][b, :, hq + h * ML_QK_DIM:hq + (h + 1) * ML_QK_DIM].astype(F32) * (ML_QK_DIM ** -0.5)
        st1[b, d, h] = dict(k=k, s=_dot_nt(q, k.astype(BF16)), qc=_dot(q, c_ref[b, d, h].astype(BF16)),
                            qn=_dot(q, n_ref[b, d, h].astype(BF16)))
    stats = _ml_stats(bds, ps_refs, ib, fb, trit3, m_ref)
    for b, d, h in units:
        r = ML_HEADS * d + h
        v = v_refs[d][b, :, h * ML_V_DIM:(h + 1) * ML_V_DIM]
        kwt = (st1[b, d, h]["k"].T * stats[b, d]["w"][r:r + 1, :]).astype(BF16)
        st1[b, d, h].update(v=v, c_loc=_dot(kwt, v), n_loc=_dot(kwt, ones))
    st2 = {}
    for b, d, h in units:
        r = ML_HEADS * d + h
        cola = stats[b, d]["cols"][:, r:r + 1]
        x = jnp.exp(jnp.where(masks[d], cola - stats[b, d]["rowb"][r:r + 1, :], -jnp.inf))
        wts = (x * st1[b, d, h]["s"]).astype(BF16)
        st2[b, d, h] = (_dot(wts, st1[b, d, h]["v"]), _dot(wts, ones))
    for b, d in bds:
        outs = []
        for h in range(ML_HEADS):
            r = ML_HEADS * d + h
            s1 = st1[b, d, h]
            sc = stats[b, d]["cols"][:, 8 + r:9 + r]
            emt = stats[b, d]["cols"][:, 16 + r:17 + r]
            wv, wo = st2[b, d, h]
            num = wv + s1["qc"] * sc
            den = wo + s1["qn"] * sc
            rden = 1.0 / jnp.maximum(jnp.abs(den), emt)
            outs.append(num * jnp.concatenate([rden] * rep, axis=1))
            a_r = stats[b, d]["a_s"][r:r + 1, :]
            s_r = stats[b, d]["s_s"][r:r + 1, :]
            c_ref[b, d, h] = (jnp.concatenate([a_r] * rep, axis=1) * c_ref[b, d, h]
                              + jnp.concatenate([s_r] * rep, axis=1) * s1["c_loc"])
            n_ref[b, d, h] = a_r * n_ref[b, d, h] + s_r * s1["n_loc"]
        h_refs[d][0, b] = jnp.concatenate(outs, axis=1).astype(BF16)
        m_ref[b, d] = stats[b, d]["m_new"]


def _mlstm(u_ml, p_big, p_small, ib, fb, tri, trit3, ctx_len):
    nb, ta, wq = u_ml.shape
    ncc, nch = ctx_len // CHUNK, ta // CHUNK
    wv = ML_HEADS * ML_V_DIM
    bb = _scan_batch(nb)
    qspecs = [pl.BlockSpec((bb, CHUNK, wq), lambda b, i: (b, _fwd_chunk(i, ncc, nch), 0)),
              pl.BlockSpec((bb, CHUNK, wq), lambda b, i: (b, _bwd_chunk(i, ncc, nch), 0))]
    v_f = pl.BlockSpec((bb, CHUNK, wv), lambda b, i: (b, _fwd_chunk(i, ncc, nch), OFF_MV // wv))
    v_b = pl.BlockSpec((bb, CHUNK, wv), lambda b, i: (b, _bwd_chunk(i, ncc, nch), OFF_MV // wv))
    ps_f = pl.BlockSpec((bb, CHUNK, LANES), lambda b, i: (b, _fwd_chunk(i, ncc, nch), 0))
    ps_b = pl.BlockSpec((bb, CHUNK, LANES), lambda b, i: (b, _bwd_chunk(i, ncc, nch), 0))
    consts = [ib, fb, tri, trit3]
    hf, hb = pl.pallas_call(
        _ml_kernel,
        grid=(nb // bb, nch),
        in_specs=qspecs + [v_f, v_b, ps_f, ps_b] + [_full_spec(a) for a in consts],
        out_specs=[pl.BlockSpec((1, bb, CHUNK, wv), lambda b, i: (0, b, _fwd_chunk(i, ncc, nch), 0)),
                   pl.BlockSpec((1, bb, CHUNK, wv), lambda b, i: (0, b, _bwd_chunk(i, ncc, nch), 0))],
        out_shape=[jax.ShapeDtypeStruct((1, nb, ta, wv), BF16)] * 2,
        scratch_shapes=[pltpu.VMEM((bb, 2, ML_HEADS, ML_QK_DIM, ML_V_DIM), F32),
                        pltpu.VMEM((bb, 2, ML_HEADS, ML_QK_DIM, LANES), F32),
                        pltpu.VMEM((bb, 2, 8, CHUNK), F32)],
        compiler_params=_cparams(("arbitrary", "arbitrary")),
        name="mlstm_scan",
    )(u_ml, u_ml, p_big, p_big, p_small, p_small, *consts)
    return hf[0], hb[0]


def _qk_prep_kernel(q_ref, k_ref, cos_ref, sa_ref, sb_ref, gq_ref, gk_ref, gm_ref, qo_ref, ko_ref, *, qscale):
    gm = gm_ref[...]
    nheads = q_ref.shape[2] // LANES
    tm = q_ref.shape[1]
    rb = _row_tile(tm, 192)

    def prep(t, g, mult, cos, sa, sb):
        t = t.astype(F32)
        sq = t * t
        sh = sq.astype(BF16)
        sm = (sq - sh.astype(F32)).astype(BF16)
        ms = (_dot(sh, gm) + _dot(sm, gm)) * (1.0 / DIFF_HEAD_DIM)
        y = t * lax.rsqrt(ms + EPS) * g
        r = y * cos + pltpu.roll(y, LANES - 16, 1) * sa + pltpu.roll(y, 16, 1) * sb
        return (r * mult).astype(BF16)

    for r in range(tm // rb):
        rows = slice(r * rb, (r + 1) * rb)
        tabs = (cos_ref[rows, :], sa_ref[rows, :], sb_ref[rows, :])
        for h in range(nheads):
            sl = slice(h * LANES, (h + 1) * LANES)
            qo_ref[0, rows, sl] = prep(q_ref[0, rows, sl], gq_ref[...], qscale, *tabs)
            ko_ref[0, rows, sl] = prep(k_ref[0, rows, sl], gk_ref[...], 1.0, *tabs)


def _qk_prep(p_big, cos, sa, sb, gq, gk, gm):
    nb, ta, _ = p_big.shape
    w = DIFF_HEADS * 2 * DIFF_HEAD_DIM
    tm = _row_tile(ta, 1056)
    qscale = DIFF_HEAD_DIM ** -0.5 * math.log2(math.e)
    row = lambda b, i: (i, 0)
    return pl.pallas_call(
        functools.partial(_qk_prep_kernel, qscale=qscale),
        grid=(nb, ta // tm),
        in_specs=[pl.BlockSpec((1, tm, w), lambda b, i: (b, i, OFF_Q // w)),
                  pl.BlockSpec((1, tm, w), lambda b, i: (b, i, OFF_K // w)),
                  pl.BlockSpec((tm, LANES), row), pl.BlockSpec((tm, LANES), row), pl.BlockSpec((tm, LANES), row),
                  _full_spec(gq), _full_spec(gk), _full_spec(gm)],
        out_specs=[pl.BlockSpec((1, tm, w), lambda b, i: (b, i, 0))] * 2,
        out_shape=[jax.ShapeDtypeStruct((nb, ta, w), BF16)] * 2,
        compiler_params=_cparams(("arbitrary", "arbitrary")),
        name="qk_prep",
    )(p_big, p_big, cos, sa, sb, gq, gk, gm)


def _attn_kernel(q_ref, k_ref, v_ref, z_ref, lam_ref, gq_ref, gk_ref, sg_ref, *rest, tk, lam_init):
    o_ref, vp_ref, e_ref = rest[-3:]
    qi = pl.program_id(2)
    nkeys = k_ref.shape[1]
    dh = DIFF_HEAD_DIM

    @pl.when(qi == 0)
    def _():
        vp_ref[:, :LANES] = v_ref[0]
        vp_ref[:, LANES:] = jnp.ones((nkeys, LANES), BF16)

    lp = lam_ref[...]
    lam = (jnp.exp(jnp.sum(lp[0:1] * lp[1:2], axis=-1, keepdims=True))
           - jnp.exp(jnp.sum(lp[2:3] * lp[3:4], axis=-1, keepdims=True)) + lam_init)
    shift = (jnp.max(jnp.abs(gq_ref[...]), axis=-1, keepdims=True)
             * jnp.max(jnp.abs(gk_ref[...]), axis=-1, keepdims=True)
             * (dh * dh ** -0.5 * math.log2(math.e)))

    q = q_ref[0]
    lane = lax.broadcasted_iota(jnp.int32, q.shape, 1)
    zero = jnp.zeros_like(q)
    acc = []
    for c, qc in enumerate((jnp.where(lane < dh, q, zero), jnp.where(lane < dh, zero, q))):
        for j in range(nkeys // tk):
            s = _dot_nt(qc, k_ref[0, j * tk:(j + 1) * tk, :])
            e_ref[c, :, j * tk:(j + 1) * tk] = jnp.exp2(s - shift).astype(BF16)
        acc.append(_dot(e_ref[c], vp_ref[...]))
    a0, a1 = acc
    o = a0[:, :LANES] / a0[:, LANES:] - lam * (a1[:, :LANES] / a1[:, LANES:])
    ms = jnp.mean(o * o, axis=-1, keepdims=True)
    o = o * lax.rsqrt(ms + EPS) * sg_ref[...] * (1.0 - lam_init)
    o_ref[0] = (o * _silu(z_ref[0].astype(F32))).astype(BF16)


def _attention(qn, kn, p_big, lam_p, gq, gk, sg, ctx_len, lam_init):
    nb, ta, w = qn.shape
    seq = ta - ctx_len
    tk = 256
    tq = 1024 if seq % 1024 == 0 else 256
    assert seq % tq == 0 and seq % ctx_len == 0 and ta % tk == 0 and ctx_len % tk == 0
    kern = functools.partial(_attn_kernel, tk=tk, lam_init=lam_init)
    full = lambda a: pl.BlockSpec(a.shape, lambda b, h, i: (0,) * a.ndim)
    consts = [lam_p, gq, gk, sg]

    def call(tq_, nkeys, row_block, key_block, nq, prev):
        in_specs = [pl.BlockSpec((1, tq_, LANES), lambda b, h, i: (b, row_block + i, h)),
                    pl.BlockSpec((1, nkeys, LANES), lambda b, h, i: (b, key_block, h)),
                    pl.BlockSpec((1, nkeys, LANES), lambda b, h, i: (b, key_block, OFF_V // LANES + h)),
                    pl.BlockSpec((1, tq_, LANES), lambda b, h, i: (b, row_block + i, OFF_ZD // LANES + h))]
        in_specs += [full(a) for a in consts]
        args = [qn, kn, p_big, p_big] + consts
        aliases = {}
        if prev is not None:
            in_specs.append(pl.BlockSpec(memory_space=pl.ANY))
            aliases = {len(args): 0}
            args.append(prev)
        return pl.pallas_call(
            kern,
            grid=(nb, DIFF_HEADS, nq),
            in_specs=in_specs,
            out_specs=pl.BlockSpec((1, tq_, LANES), lambda b, h, i: (b, row_block + i, h)),
            out_shape=jax.ShapeDtypeStruct((nb, ta, w), BF16),
            scratch_shapes=[pltpu.VMEM((nkeys, 2 * LANES), BF16), pltpu.VMEM((2, tq_, nkeys), BF16)],
            input_output_aliases=aliases,
            compiler_params=_cparams(("arbitrary", "arbitrary", "arbitrary")),
            name="diff_attention" if prev is None else "diff_attention_ctx",
        )(*args)

    yd = call(tq, ta, 0, 0, seq // tq, None)
    return call(ctx_len, ctx_len, seq // ctx_len, seq // ctx_len, 1, yd)


def _merge_kernel(x_ref, mod_ref, yf_ref, yb_ref, zs_ref, yd_ref, hf_ref, hb_ref, mo_ref, mz_ref, gt_ref,
                  sng_ref, mng_ref, wb_ref, wo_ref, o_ref, *, lat_len, ctx_row):
    b, i = pl.program_id(0), pl.program_id(1)
    tm, d = x_ref.shape[1], x_ref.shape[2]

    pd = _dot(yd_ref[0], wb_ref[1])
    ya = (yf_ref[0].astype(F32) + yb_ref[0].astype(F32)) * _silu(zs_ref[0].astype(F32))
    ya = ya * lax.rsqrt(jnp.mean(ya * ya, axis=-1, keepdims=True) + EPS) * sng_ref[...]
    pa = _dot(ya.astype(BF16), wb_ref[0])

    hm = (hf_ref[0].astype(F32) + hb_ref[0].astype(F32)) * _sigmoid(mo_ref[0].astype(F32))
    parts = []
    for h in range(ML_HEADS):
        t = hm[:, h * ML_V_DIM:(h + 1) * ML_V_DIM]
        parts.append(t * lax.rsqrt(jnp.mean(t * t, axis=-1, keepdims=True) + EPS))
    yc = jnp.concatenate(parts, axis=1) * mng_ref[...] * _silu(mz_ref[0].astype(F32))

    g = _sigmoid(gt_ref[0].astype(F32))
    mixed = (g[:, :d] * pa + g[:, d:2 * d] * pd + g[:, 2 * d:] * _dot(yc.astype(BF16), wb_ref[2]))
    out = _dot(mixed.astype(BF16), wo_ref[...])

    row = i * tm + lax.broadcasted_iota(jnp.int32, (tm, 1), 0)
    is_ctx = row >= lat_len
    gate = jnp.where(is_ctx, mod_ref[pl.ds(ctx_row, 1), 2 * d:], mod_ref[pl.ds(b, 1), 2 * d:])
    o_ref[0] = x_ref[0] + gate * out


def _merge(xall, mod_l, yf, yb, p_big, yd, hf, hb, sng, mng, wb, wo, ctx_len, out_rows):
    nb, ta, d = xall.shape
    tm = _row_tile(out_rows, 512)
    kern = functools.partial(_merge_kernel, lat_len=ta - ctx_len, ctx_row=nb)
    blk = lambda col: pl.BlockSpec((1, tm, d), lambda b, i: (b, i, col))
    full = lambda a: pl.BlockSpec(a.shape, lambda b, i: (0,) * a.ndim)
    return pl.pallas_call(
        kern,
        grid=(nb, out_rows // tm),
        in_specs=[blk(0), full(mod_l), blk(0), blk(0), blk(OFF_ZS // d), blk(0), blk(0), blk(0),
                  blk(OFF_MO // d), blk(OFF_MZ // d),
                  pl.BlockSpec((1, tm, 3 * d), lambda b, i: (b, i, OFF_GATES // (3 * d))),
                  full(sng), full(mng), full(wb), full(wo)],
        out_specs=blk(0),
        out_shape=jax.ShapeDtypeStruct((nb, out_rows, d), F32),
        compiler_params=_cparams(("arbitrary", "arbitrary")),
        name="merge",
    )(xall, mod_l, yf, yb, p_big, yd, hf, hb, p_big, p_big, p_big, sng, mng, wb, wo)


def _rope_tables(ctx_len, seq):
    rows = seq // GRID_W
    row = jnp.repeat(jnp.arange(rows, dtype=F32), GRID_W)
    col = jnp.tile(jnp.arange(GRID_W, dtype=F32), rows)
    half = DIFF_HEAD_DIM // 2
    inv_freq = ROPE_BASE ** (-jnp.arange(0, half, 2, dtype=F32) / half)
    ang_r = row[:, None] * inv_freq
    ang_c = col[:, None] * inv_freq
    cos = jnp.concatenate([jnp.cos(ang_r), jnp.cos(ang_r), jnp.cos(ang_c), jnp.cos(ang_c)], axis=-1)
    sin = jnp.concatenate([jnp.sin(ang_r), jnp.sin(ang_r), jnp.sin(ang_c), jnp.sin(ang_c)], axis=-1)
    cos = jnp.concatenate([cos, jnp.ones((ctx_len, DIFF_HEAD_DIM), F32)], axis=0)
    sin = jnp.concatenate([sin, jnp.zeros((ctx_len, DIFF_HEAD_DIM), F32)], axis=0)
    cos = jnp.tile(cos, (1, LANES // DIFF_HEAD_DIM))
    sin = jnp.tile(sin, (1, LANES // DIFF_HEAD_DIM))
    first = (jnp.arange(LANES) % (DIFF_HEAD_DIM // 2)) < DIFF_HEAD_DIM // 4
    return cos, jnp.where(first, -sin, 0.0), jnp.where(first, 0.0, sin)


def _scan_constants():
    idx = np.arange(CHUNK)
    tri_f = (idx[None, :] <= idx[:, None]).astype(np.float32)
    tri = np.stack([tri_f, tri_f.T])
    trit = np.stack([tri_f.T, tri_f])
    e01 = np.zeros((2, LANES, SSD_HEADS * SSD_HEAD_DIM), np.float32)
    for d in range(2):
        for h in range(SSD_HEADS):
            e01[d, SSD_HEADS * d + h, h * SSD_HEAD_DIM:(h + 1) * SSD_HEAD_DIM] = 1.0
    half = np.arange(LANES) // DIFF_HEAD_DIM
    gm = (half[:, None] == half[None, :]).astype(np.float32)
    tri3 = np.concatenate([tri] * 3, axis=2)
    trit3 = np.concatenate([trit] * 3, axis=1)
    e01x2 = np.concatenate([e01] * 2, axis=1)
    return (jnp.asarray(tri, BF16), jnp.asarray(tri3, BF16), jnp.asarray(trit3, BF16), jnp.asarray(e01x2, BF16),
            jnp.asarray(gm, BF16))


def _pad_row(v, offset):
    v = v.reshape(1, -1).astype(F32)
    return jnp.pad(v, ((0, 0), (offset, LANES - offset - v.shape[1])))


def kernel(x, c, ctx, c_ctx, w_mod, b_mod, norm_g, w_in, ssd_conv_w, ssd_conv_b, ssd_a_log, ssd_dt_bias, ssd_d,
           ssd_norm_g, diff_qn_g, diff_kn_g, diff_lambda, diff_subln_g, ml_conv_w, ml_conv_b, ml_i_bias,
           ml_f_bias, ml_norm_g, w_branch, w_out):
    nb, seq, d = x.shape
    ctx_len = ctx.shape[1]
    depth = w_mod.shape[0]
    assert d == 1024 and ctx_len % CHUNK == 0 and seq % CHUNK == 0 and seq % GRID_W == 0 and nb < 8

    sizes = (1536, 1024, 32, 1024, 1024, 1024, 1024, 1024, 1024, 1024, 1024, 8, 8, 3072)
    offs = np.concatenate([[0], np.cumsum(sizes)])
    w_in16 = w_in.astype(BF16)
    seg = lambda n: w_in16[:, :, offs[n]:offs[n + 1]]
    w_big = jnp.concatenate([seg(1), seg(3), seg(4), seg(5), seg(6), seg(7), seg(8), seg(9), seg(10),
                             seg(13), seg(0)], axis=-1)
    w_small = jnp.concatenate([seg(2), seg(11), seg(12)], axis=-1)
    w_small = jnp.pad(w_small, ((0, 0), (0, 0), (0, LANES - w_small.shape[-1])))

    cc = jnp.concatenate([c, c_ctx[None, :], jnp.zeros((8 - nb - 1, d), F32)], axis=0)
    mod = _modulation(cc, w_mod, b_mod)

    tri, tri3, trit3, e01, gm = _scan_constants()
    cos, sa, sb = _rope_tables(ctx_len, seq)
    wb = w_branch.astype(BF16)
    wo = w_out.astype(BF16)
    xall = jnp.concatenate([x, ctx], axis=1)
    ta = seq + ctx_len

    for l in range(depth):
        lam_init = 0.8 - 0.6 * math.exp(-0.3 * l)
        p_big, p_small = _inproj(xall, mod[l], norm_g[l][None, :], w_big[l], w_small[l], ctx_len)
        u_ssd, u_ml = _conv(p_big, ssd_conv_w[l], ssd_conv_b[l][None, :], ml_conv_w[l], ml_conv_b[l][None, :],
                            ctx_len)
        yf, yb = _ssd(u_ssd, p_small, _pad_row(ssd_dt_bias[l], 0), _pad_row(-jnp.exp(ssd_a_log[l]), 0),
                      jnp.repeat(ssd_d[l], SSD_HEAD_DIM)[None, :], tri, tri3, trit3, e01, ctx_len)
        qn, kn = _qk_prep(p_big, cos, sa, sb, jnp.tile(diff_qn_g[l], 2)[None, :],
                          jnp.tile(diff_kn_g[l], 2)[None, :], gm)
        yd = _attention(qn, kn, p_big, diff_lambda[l], diff_qn_g[l][None, :], diff_kn_g[l][None, :],
                        diff_subln_g[l][None, :], ctx_len, lam_init)
        hf, hb = _mlstm(u_ml, p_big, p_small, _pad_row(ml_i_bias[l], SM_IG), _pad_row(ml_f_bias[l], SM_FG),
                        tri, trit3, ctx_len)
        xall = _merge(xall, mod[l], yf, yb, p_big, yd, hf, hb, ssd_norm_g[l][None, :], ml_norm_g[l][None, :],
                      wb[l], wo[l], ctx_len, ta if l + 1 < depth else seq)
    return xall
```

```python
import functools
import math

import numpy as np
import jax
import jax.numpy as jnp
from jax import lax
from jax.experimental import pallas as pl
from jax.experimental.pallas import tpu as pltpu

F32 = jnp.float32
BF16 = jnp.bfloat16

GRID_W = 64
EPS = 1e-6
CONV_K = 3
SSD_HEADS = 16
SSD_HEAD_DIM = 64
SSD_GROUPS = 2
SSD_STATE = 128
DIFF_HEADS = 8
DIFF_HEAD_DIM = 64
ROPE_BASE = 10000.0
ML_HEADS = 4
ML_QK_DIM = 128
ML_V_DIM = 256
CHUNK = 128
LANES = 128
BF16_ROWS = 16

OFF_ZS, OFF_Q, OFF_K, OFF_V, OFF_ZD = 0, 1024, 2048, 3072, 4096
OFF_MQK, OFF_MV, OFF_MO, OFF_MZ, OFF_GATES, OFF_XBC = 5120, 6144, 7168, 8192, 9216, 12288
N_BIG = 13824
SM_IG, SM_FG = 32, 40

VMEM_LIMIT = 56 * 1024 * 1024


def _cparams(sem):
    return pltpu.CompilerParams(dimension_semantics=sem, vmem_limit_bytes=VMEM_LIMIT)


def _split3(v):
    h = v.astype(BF16)
    r = v - h.astype(F32)
    m = r.astype(BF16)
    l = (r - m.astype(F32)).astype(BF16)
    return h, m, l


def _dot(a, b):
    return jnp.dot(a, b, preferred_element_type=F32)


def _dot_nt(a, b):
    return lax.dot_general(a, b, (((1,), (1,)), ((), ())), preferred_element_type=F32)


def _dot_tn(a, b):
    return lax.dot_general(a, b, (((0,), (0,)), ((), ())), preferred_element_type=F32)


def _dot_exact_rhs01(v, m01x3):
    return _dot(jnp.concatenate(_split3(v), axis=1), m01x3)


def _expand_rhs01(v, m01x2):
    h = v.astype(BF16)
    m = (v - h.astype(F32)).astype(BF16)
    return _dot(jnp.concatenate([h, m], axis=1), m01x2)


def _dot_exact_lhs01(m01x3, v):
    return _dot(m01x3, jnp.concatenate(_split3(v), axis=0))


def _sigmoid(x):
    return 1.0 / (1.0 + jnp.exp2(x * (-math.log2(math.e))))


def _silu(x):
    return x * _sigmoid(x)


def _softplus(x):
    return jnp.maximum(x, 0.0) + jnp.log(1.0 + jnp.exp(-jnp.abs(x)))


def _mod_kernel(cc_ref, w_ref, b_ref, o_ref):
    a = _silu(cc_ref[...])
    w = w_ref[0]
    ah, am, _ = _split3(a)
    wh = w.astype(BF16)
    wm = (w - wh.astype(F32)).astype(BF16)
    o_ref[0] = _dot(ah, wh) + _dot(am, wh) + _dot(ah, wm) + b_ref[0]


def _modulation(cc, w_mod, b_mod):
    depth, d, d3 = w_mod.shape
    tn = 1024
    return pl.pallas_call(
        _mod_kernel,
        grid=(depth, d3 // tn),
        in_specs=[pl.BlockSpec((8, d), lambda l, j: (0, 0)),
                  pl.BlockSpec((1, d, tn), lambda l, j: (l, 0, j)),
                  pl.BlockSpec((1, 1, tn), lambda l, j: (l, 0, j))],
        out_specs=pl.BlockSpec((1, 8, tn), lambda l, j: (l, 0, j)),
        out_shape=jax.ShapeDtypeStruct((depth, 8, d3), F32),
        compiler_params=_cparams(("arbitrary", "arbitrary")),
        name="modulation",
    )(cc, w_mod, b_mod.reshape(depth, 1, d3))


def _inproj_kernel(x_ref, mod_ref, g_ref, w_ref, ws_ref, o_ref, os_ref, h_ref, *, lat_len, ctx_row):
    b, i, j = pl.program_id(0), pl.program_id(1), pl.program_id(2)
    tm, d = h_ref.shape

    @pl.when(j == 0)
    def _():
        x = x_ref[0]
        ms = jnp.mean(x * x, axis=-1, keepdims=True)
        y = x * lax.rsqrt(ms + EPS) * g_ref[...]
        row = i * tm + lax.broadcasted_iota(jnp.int32, (tm, 1), 0)
        is_ctx = row >= lat_len
        mlat = mod_ref[pl.ds(b, 1), :]
        mctx = mod_ref[pl.ds(ctx_row, 1), :]
        shift = jnp.where(is_ctx, mctx[:, :d], mlat[:, :d])
        scale = jnp.where(is_ctx, mctx[:, d:2 * d], mlat[:, d:2 * d])
        h = (y * (1.0 + scale) + shift).astype(BF16)
        h_ref[...] = h
        os_ref[0] = _dot(h, ws_ref[...])

    o_ref[0] = _dot(h_ref[...], w_ref[...]).astype(BF16)


def _row_tile(ta, target):
    best = BF16_ROWS
    for t in range(BF16_ROWS, target + 1, BF16_ROWS):
        if ta % t == 0:
            best = t
    return best


def _inproj(xall, mod_l, g, w_big, w_small, ctx_len):
    nb, ta, d = xall.shape
    tm = _row_tile(ta, 1056)
    tn = 3456
    kern = functools.partial(_inproj_kernel, lat_len=ta - ctx_len, ctx_row=nb)
    return pl.pallas_call(
        kern,
        grid=(nb, ta // tm, N_BIG // tn),
        in_specs=[pl.BlockSpec((1, tm, d), lambda b, i, j: (b, i, 0)),
                  pl.BlockSpec(mod_l.shape, lambda b, i, j: (0, 0)),
                  pl.BlockSpec((1, d), lambda b, i, j: (0, 0)),
                  pl.BlockSpec((d, tn), lambda b, i, j: (0, j)),
                  pl.BlockSpec((d, LANES), lambda b, i, j: (0, 0))],
        out_specs=[pl.BlockSpec((1, tm, tn), lambda b, i, j: (b, i, j)),
                   pl.BlockSpec((1, tm, LANES), lambda b, i, j: (b, i, 0))],
        out_shape=[jax.ShapeDtypeStruct((nb, ta, N_BIG), BF16),
                   jax.ShapeDtypeStruct((nb, ta, LANES), F32)],
        scratch_shapes=[pltpu.VMEM((tm, d), BF16)],
        compiler_params=_cparams(("arbitrary", "arbitrary", "arbitrary")),
        name="inproj",
    )(xall, mod_l, g, w_big, w_small)


def _scan_batch(nb):
    return 4 if nb % 4 == 0 else (2 if nb % 2 == 0 else 1)


def _fwd_chunk(i, ncc, nch):
    return jnp.where(i < ncc, nch - ncc + i, i - ncc)


def _bwd_chunk(i, ncc, nch):
    del ncc
    return nch - 1 - i


def _full_spec(a):
    nd = a.ndim
    return pl.BlockSpec(a.shape, lambda b, i: (0,) * nd)


def _conv_kernel(xs_ref, xsp_ref, xsn_ref, xm_ref, xmp_ref, xmn_ref, cws_ref, cbs_ref, cwm_ref, cbm_ref,
                 us_ref, um_ref, *, lat_len, ta):
    tm = xs_ref.shape[1]
    row0 = pl.program_id(1) * tm
    has_prev = jnp.logical_and(row0 != 0, row0 != lat_len)
    has_next = jnp.logical_and(row0 + tm != lat_len, row0 + tm != ta)
    ri = lax.broadcasted_iota(jnp.int32, (tm, tm), 0)
    ci = lax.broadcasted_iota(jnp.int32, (tm, tm), 1)
    down = (ri == ci + 1).astype(BF16)
    up = (ri + 1 == ci).astype(BF16)
    sub = lax.broadcasted_iota(jnp.int32, (8, LANES), 0)
    for x_ref, xp_ref, xn_ref, cw_ref, cb_ref, o_ref in ((xs_ref, xsp_ref, xsn_ref, cws_ref, cbs_ref, us_ref),
                                                         (xm_ref, xmp_ref, xmn_ref, cwm_ref, cbm_ref, um_ref)):
        xb = x_ref[0]
        xprev_all = _dot(down, xb)
        xnext_all = _dot(up, xb)
        for j in range(x_ref.shape[2] // LANES):
            sl = slice(j * LANES, (j + 1) * LANES)
            w0, w1, w2 = cw_ref[0:1, sl], cw_ref[1:2, sl], cw_ref[2:3, sl]
            u = xprev_all[:, sl] * w0 + xb[:, sl].astype(F32) * w1 + xnext_all[:, sl] * w2 + cb_ref[:, sl]
            prow = jnp.where(has_prev, xp_ref[0, BF16_ROWS - 1:BF16_ROWS, sl].astype(F32), 0.0) * w0
            nrow = jnp.where(has_next, xn_ref[0, 0:1, sl].astype(F32), 0.0) * w2
            u = jnp.concatenate([u[:8] + jnp.where(sub == 0, prow, 0.0), u[8:tm - 8],
                                 u[tm - 8:] + jnp.where(sub == 7, nrow, 0.0)], axis=0)
            o_ref[0, :, sl] = _silu(u).astype(BF16)


def _conv(p_big, cws, cbs, cwm, cbm, ctx_len):
    nb, ta, _ = p_big.shape
    tm = _row_tile(math.gcd(ta - ctx_len, ctx_len), 256)
    per = tm // BF16_ROWS
    nrow16 = ta // BF16_ROWS
    ws, wm = cws.shape[1], cwm.shape[1]

    def specs(width, col_block):
        return [pl.BlockSpec((1, tm, width), lambda b, i: (b, i, col_block)),
                pl.BlockSpec((1, BF16_ROWS, width), lambda b, i: (b, jnp.maximum(i * per - 1, 0), col_block)),
                pl.BlockSpec((1, BF16_ROWS, width),
                             lambda b, i: (b, jnp.minimum((i + 1) * per, nrow16 - 1), col_block))]

    consts = [cws, cbs, cwm, cbm]
    return pl.pallas_call(
        functools.partial(_conv_kernel, lat_len=ta - ctx_len, ta=ta),
        grid=(nb, ta // tm),
        in_specs=specs(ws, OFF_XBC // ws) + specs(wm, OFF_MQK // wm) + [_full_spec(a) for a in consts],
        out_specs=[pl.BlockSpec((1, tm, ws), lambda b, i: (b, i, 0)),
                   pl.BlockSpec((1, tm, wm), lambda b, i: (b, i, 0))],
        out_shape=[jax.ShapeDtypeStruct((nb, ta, ws), BF16), jax.ShapeDtypeStruct((nb, ta, wm), BF16)],
        compiler_params=_cparams(("arbitrary", "arbitrary")),
        name="conv_silu",
    )(p_big, p_big, p_big, p_big, p_big, p_big, *consts)


def _ssd_decay_sums(d, ps, dtb_ref, a_ref, tri3_ref, trit3_ref):
    dt = _softplus(ps + dtb_ref[...])
    a = dt * a_ref[...]
    acs = _dot_exact_lhs01(tri3_ref[d], a)
    acs_t = _dot_exact_rhs01(a.T, trit3_ref[d])
    return dict(dt=dt, acs=acs, acs_t=acs_t)


def _ssd_expansions(d, sd, e_ref):
    L = CHUNK
    last = L - 1 if d == 0 else 0
    dend = jnp.exp(sd["acs"][last:last + 1, :] - sd["acs"])
    e01 = e_ref[d]
    eacs_e = _expand_rhs01(jnp.exp(sd["acs"]), e01)
    sd.update(dt_e=_expand_rhs01(sd["dt"], e01), dtend_e=_expand_rhs01(sd["dt"] * dend, e01),
              eacs_e=eacs_e, cd_e=eacs_e[last:last + 1, :])


def _ssd_kernel(uf, ub, psf, psb, dtb, aneg, dsk, tri, tri3, trit3, e01, yf, yb, s_ref):
    @pl.when(pl.program_id(1) == 0)
    def _():
        s_ref[...] = jnp.zeros_like(s_ref)

    L = CHUNK
    hp = SSD_HEADS * SSD_HEAD_DIM
    gn = SSD_GROUPS * SSD_STATE
    gw = hp // SSD_GROUPS
    hpg = SSD_HEADS // SSD_GROUPS
    u_refs, y_refs, ps_refs = (uf, ub), (yf, yb), (psf, psb)
    bds = [(b, d) for b in range(uf.shape[0]) for d in range(2)]
    dgs = [(b, d, g) for b, d in bds for g in range(SSD_GROUPS)]

    cg, bg, sg, yo = {}, {}, {}, {}
    for b, d, g in dgs:
        bg[b, d, g] = u_refs[d][b, :, hp + g * SSD_STATE:hp + (g + 1) * SSD_STATE]
        cg[b, d, g] = u_refs[d][b, :, hp + gn + g * SSD_STATE:hp + gn + (g + 1) * SSD_STATE]
        sg[b, d, g] = _dot_nt(cg[b, d, g], bg[b, d, g])
        yo[b, d, g] = _dot(cg[b, d, g], s_ref[b, d, g].astype(BF16))
    st = {(b, d): _ssd_decay_sums(d, ps_refs[d][b], dtb, aneg, tri3, trit3) for b, d in bds}
    for b, d in bds:
        _ssd_expansions(d, st[b, d], e01)
    xs = {(b, d): u_refs[d][b, :, :hp].astype(F32) for b, d in bds}
    masks = [tri[d] > 0 for d in range(2)]
    left = lax.broadcasted_iota(jnp.int32, (L, LANES), 1) < SSD_HEAD_DIM

    pairs, snew = {}, {}
    for b, d, g in dgs:
        sd = st[b, d]
        xd = xs[b, d][:, g * gw:(g + 1) * gw] * sd["dt_e"][:, g * gw:(g + 1) * gw]
        for k in range(hpg // 2):
            ms = []
            for h in (g * hpg + 2 * k, g * hpg + 2 * k + 1):
                j = SSD_HEADS * d + h
                seg = sd["acs"][:, j:j + 1] - sd["acs_t"][j:j + 1, :]
                ms.append((sg[b, d, g] * jnp.exp(jnp.where(masks[d], seg, -jnp.inf))).astype(BF16))
            xp = xd[:, 2 * k * SSD_HEAD_DIM:(2 * k + 2) * SSD_HEAD_DIM]
            rhs = jnp.concatenate([jnp.where(left, xp, 0.0), jnp.where(left, 0.0, xp)], axis=0).astype(BF16)
            pairs[b, d, g, k] = _dot(jnp.concatenate(ms, axis=1), rhs)
        xde = (xs[b, d][:, g * gw:(g + 1) * gw] * sd["dtend_e"][:, g * gw:(g + 1) * gw]).astype(BF16)
        snew[b, d, g] = _dot_tn(bg[b, d, g], xde)

    for b, d in bds:
        ys = []
        for g in range(SSD_GROUPS):
            sl = slice(g * gw, (g + 1) * gw)
            ys.append(jnp.concatenate([pairs[b, d, g, k] for k in range(hpg // 2)], axis=1)
                      + yo[b, d, g] * st[b, d]["eacs_e"][:, sl])
            s_ref[b, d, g] = s_ref[b, d, g] * st[b, d]["cd_e"][:, sl] + snew[b, d, g]
        y = jnp.concatenate(ys, axis=1)
        if d == 0:
            y = y + dsk[...] * xs[b, d]
        y_refs[d][0, b] = y.astype(BF16)


def _ssd(u_ssd, p_small, dtb, aneg, dsk, tri, tri3, trit3, e01, ctx_len):
    nb, ta, width = u_ssd.shape
    ncc, nch = ctx_len // CHUNK, ta // CHUNK
    hp = SSD_HEADS * SSD_HEAD_DIM
    bb = _scan_batch(nb)
    xspecs = [pl.BlockSpec((bb, CHUNK, width), lambda b, i: (b, _fwd_chunk(i, ncc, nch), 0)),
              pl.BlockSpec((bb, CHUNK, width), lambda b, i: (b, _bwd_chunk(i, ncc, nch), 0))]
    ps_f = pl.BlockSpec((bb, CHUNK, LANES), lambda b, i: (b, _fwd_chunk(i, ncc, nch), 0))
    ps_b = pl.BlockSpec((bb, CHUNK, LANES), lambda b, i: (b, _bwd_chunk(i, ncc, nch), 0))
    consts = [dtb, aneg, dsk, tri, tri3, trit3, e01]
    yf, yb = pl.pallas_call(
        _ssd_kernel,
        grid=(nb // bb, nch),
        in_specs=xspecs + [ps_f, ps_b] + [_full_spec(a) for a in consts],
        out_specs=[pl.BlockSpec((1, bb, CHUNK, hp), lambda b, i: (0, b, _fwd_chunk(i, ncc, nch), 0)),
                   pl.BlockSpec((1, bb, CHUNK, hp), lambda b, i: (0, b, _bwd_chunk(i, ncc, nch), 0))],
        out_shape=[jax.ShapeDtypeStruct((1, nb, ta, hp), BF16)] * 2,
        scratch_shapes=[pltpu.VMEM((bb, 2, SSD_GROUPS, SSD_STATE, hp // SSD_GROUPS), F32)],
        compiler_params=_cparams(("arbitrary", "arbitrary")),
        name="ssd_scan",
    )(u_ssd, u_ssd, p_small, p_small, *consts)
    return yf[0], yb[0]


def _ml_stats(d, ps, ib_ref, fb_ref, trit3_ref, m_prev):
    L = CHUNK
    li_t = (ps + ib_ref[...]).T
    lf_t = (-_softplus(-(ps + fb_ref[...]))).T
    bc = _dot_exact_rhs01(lf_t, trit3_ref[d])[SM_FG:SM_FG + 8, :]
    lir = li_t[SM_IG:SM_IG + 8, :]
    last = L - 1 if d == 0 else 0
    lane = lax.broadcasted_iota(jnp.int32, (8, L), 1)
    b_last = jnp.broadcast_to(bc[:, last:last + 1], (8, L))

    gl = b_last - bc + lir
    m_loc = jnp.broadcast_to(jnp.max(gl, axis=-1, keepdims=True), (8, L))
    m_new = jnp.maximum(b_last + m_prev, m_loc)

    g = lir - bc
    sh = 1
    while sh < L:
        if d == 0:
            g = jnp.maximum(g, jnp.where(lane >= sh, pltpu.roll(g, sh, 1), -jnp.inf))
        else:
            g = jnp.maximum(g, jnp.where(lane < L - sh, pltpu.roll(g, L - sh, 1), -jnp.inf))
        sh *= 2
    e = bc + m_prev
    m_t = jnp.maximum(e, bc + g)
    cols = jnp.concatenate([bc - m_t, jnp.exp(e - m_t), jnp.exp(-m_t), jnp.zeros((L - 24, L), F32)], axis=0).T
    return dict(w=jnp.exp(gl - m_loc), a_s=jnp.exp(b_last + m_prev - m_new), s_s=jnp.exp(m_loc - m_new),
                m_new=m_new, rowb=bc - lir, cols=cols)


def _ml_kernel(uf, ub, vf, vb, psf, psb, ib, fb, tri, trit3, hf, hb, c_ref, n_ref, m_ref):
    @pl.when(pl.program_id(1) == 0)
    def _():
        c_ref[...] = jnp.zeros_like(c_ref)
        n_ref[...] = jnp.zeros_like(n_ref)
        m_ref[...] = jnp.zeros_like(m_ref)

    L = CHUNK
    hq = ML_HEADS * ML_QK_DIM
    rep = ML_V_DIM // LANES
    ones = jnp.ones((L, LANES), BF16)
    u_refs, v_refs, h_refs, ps_refs = (uf, ub), (vf, vb), (hf, hb), (psf, psb)
    masks = [tri[d] > 0 for d in range(2)]
    bds = [(b, d) for b in range(uf.shape[0]) for d in range(2)]
    units = [(b, d, h) for b, d in bds for h in range(ML_HEADS)]

    st1 = {}
    for b, d, h in units:
        q = u_refs[d][b, :, h * ML_QK_DIM:(h + 1) * ML_QK_DIM]
        k = u_refs[d][b, :, hq + h * ML_QK_DIM:hq + (h + 1) * ML_QK_DIM].astype(F32) * (ML_QK_DIM ** -0.5)
        st1[b, d, h] = dict(k=k, s=_dot_nt(q, k.astype(BF16)), qc=_dot(q, c_ref[b, d, h].astype(BF16)),
                            qn=_dot(q, n_ref[b, d, h].astype(BF16)))
    stats = {(b, d): _ml_stats(d, ps_refs[d][b], ib, fb, trit3, m_ref[b, d]) for b, d in bds}
    for b, d, h in units:
        r = ML_HEADS * d + h
        v = v_refs[d][b, :, h * ML_V_DIM:(h + 1) * ML_V_DIM]
        kwt = (st1[b, d, h]["k"].T * stats[b, d]["w"][r:r + 1, :]).astype(BF16)
        st1[b, d, h].update(v=v, c_loc=_dot(kwt, v), n_loc=_dot(kwt, ones))
    st2 = {}
    for b, d, h in units:
        r = ML_HEADS * d + h
        cola = stats[b, d]["cols"][:, r:r + 1]
        x = jnp.exp(jnp.where(masks[d], cola - stats[b, d]["rowb"][r:r + 1, :], -jnp.inf))
        wts = (x * st1[b, d, h]["s"]).astype(BF16)
        st2[b, d, h] = (_dot(wts, st1[b, d, h]["v"]), _dot(wts, ones))
    for b, d in bds:
        outs = []
        for h in range(ML_HEADS):
            r = ML_HEADS * d + h
            s1 = st1[b, d, h]
            sc = stats[b, d]["cols"][:, 8 + r:9 + r]
            emt = stats[b, d]["cols"][:, 16 + r:17 + r]
            wv, wo = st2[b, d, h]
            num = wv + s1["qc"] * sc
            den = wo + s1["qn"] * sc
            rden = 1.0 / jnp.maximum(jnp.abs(den), emt)
            outs.append(num * jnp.concatenate([rden] * rep, axis=1))
            a_r = stats[b, d]["a_s"][r:r + 1, :]
            s_r = stats[b, d]["s_s"][r:r + 1, :]
            c_ref[b, d, h] = (jnp.concatenate([a_r] * rep, axis=1) * c_ref[b, d, h]
                              + jnp.concatenate([s_r] * rep, axis=1) * s1["c_loc"])
            n_ref[b, d, h] = a_r * n_ref[b, d, h] + s_r * s1["n_loc"]
        h_refs[d][0, b] = jnp.concatenate(outs, axis=1).astype(BF16)
        m_ref[b, d] = stats[b, d]["m_new"]


def _mlstm(u_ml, p_big, p_small, ib, fb, tri, trit3, ctx_len):
    nb, ta, wq = u_ml.shape
    ncc, nch = ctx_len // CHUNK, ta // CHUNK
    wv = ML_HEADS * ML_V_DIM
    bb = _scan_batch(nb)
    qspecs = [pl.BlockSpec((bb, CHUNK, wq), lambda b, i: (b, _fwd_chunk(i, ncc, nch), 0)),
              pl.BlockSpec((bb, CHUNK, wq), lambda b, i: (b, _bwd_chunk(i, ncc, nch), 0))]
    v_f = pl.BlockSpec((bb, CHUNK, wv), lambda b, i: (b, _fwd_chunk(i, ncc, nch), OFF_MV // wv))
    v_b = pl.BlockSpec((bb, CHUNK, wv), lambda b, i: (b, _bwd_chunk(i, ncc, nch), OFF_MV // wv))
    ps_f = pl.BlockSpec((bb, CHUNK, LANES), lambda b, i: (b, _fwd_chunk(i, ncc, nch), 0))
    ps_b = pl.BlockSpec((bb, CHUNK, LANES), lambda b, i: (b, _bwd_chunk(i, ncc, nch), 0))
    consts = [ib, fb, tri, trit3]
    hf, hb = pl.pallas_call(
        _ml_kernel,
        grid=(nb // bb, nch),
        in_specs=qspecs + [v_f, v_b, ps_f, ps_b] + [_full_spec(a) for a in consts],
        out_specs=[pl.BlockSpec((1, bb, CHUNK, wv), lambda b, i: (0, b, _fwd_chunk(i, ncc, nch), 0)),
                   pl.BlockSpec((1, bb, CHUNK, wv), lambda b, i: (0, b, _bwd_chunk(i, ncc, nch), 0))],
        out_shape=[jax.ShapeDtypeStruct((1, nb, ta, wv), BF16)] * 2,
        scratch_shapes=[pltpu.VMEM((bb, 2, ML_HEADS, ML_QK_DIM, ML_V_DIM), F32),
                        pltpu.VMEM((bb, 2, ML_HEADS, ML_QK_DIM, LANES), F32),
                        pltpu.VMEM((bb, 2, 8, CHUNK), F32)],
        compiler_params=_cparams(("arbitrary", "arbitrary")),
        name="mlstm_scan",
    )(u_ml, u_ml, p_big, p_big, p_small, p_small, *consts)
    return hf[0], hb[0]


def _qk_prep_kernel(q_ref, k_ref, cos_ref, sa_ref, sb_ref, gq_ref, gk_ref, gm_ref, qo_ref, ko_ref, *, qscale):
    gm = gm_ref[...]
    nheads = q_ref.shape[2] // LANES
    tm = q_ref.shape[1]
    rb = _row_tile(tm, 192)

    def prep(t, g, mult, cos, sa, sb):
        t = t.astype(F32)
        sq = t * t
        sh = sq.astype(BF16)
        sm = (sq - sh.astype(F32)).astype(BF16)
        ms = (_dot(sh, gm) + _dot(sm, gm)) * (1.0 / DIFF_HEAD_DIM)
        y = t * lax.rsqrt(ms + EPS) * g
        r = y * cos + pltpu.roll(y, LANES - 16, 1) * sa + pltpu.roll(y, 16, 1) * sb
        return (r * mult).astype(BF16)

    for r in range(tm // rb):
        rows = slice(r * rb, (r + 1) * rb)
        tabs = (cos_ref[rows, :], sa_ref[rows, :], sb_ref[rows, :])
        for h in range(nheads):
            sl = slice(h * LANES, (h + 1) * LANES)
            qo_ref[0, rows, sl] = prep(q_ref[0, rows, sl], gq_ref[...], qscale, *tabs)
            ko_ref[0, rows, sl] = prep(k_ref[0, rows, sl], gk_ref[...], 1.0, *tabs)


def _qk_prep(p_big, cos, sa, sb, gq, gk, gm):
    nb, ta, _ = p_big.shape
    w = DIFF_HEADS * 2 * DIFF_HEAD_DIM
    tm = _row_tile(ta, 1056)
    qscale = DIFF_HEAD_DIM ** -0.5 * math.log2(math.e)
    row = lambda b, i: (i, 0)
    return pl.pallas_call(
        functools.partial(_qk_prep_kernel, qscale=qscale),
        grid=(nb, ta // tm),
        in_specs=[pl.BlockSpec((1, tm, w), lambda b, i: (b, i, OFF_Q // w)),
                  pl.BlockSpec((1, tm, w), lambda b, i: (b, i, OFF_K // w)),
                  pl.BlockSpec((tm, LANES), row), pl.BlockSpec((tm, LANES), row), pl.BlockSpec((tm, LANES), row),
                  _full_spec(gq), _full_spec(gk), _full_spec(gm)],
        out_specs=[pl.BlockSpec((1, tm, w), lambda b, i: (b, i, 0))] * 2,
        out_shape=[jax.ShapeDtypeStruct((nb, ta, w), BF16)] * 2,
        compiler_params=_cparams(("arbitrary", "arbitrary")),
        name="qk_prep",
    )(p_big, p_big, cos, sa, sb, gq, gk, gm)


def _attn_kernel(q_ref, k_ref, v_ref, z_ref, lam_ref, gq_ref, gk_ref, sg_ref, *rest, tk, lam_init):
    o_ref, vp_ref, e_ref = rest[-3:]
    qi = pl.program_id(2)
    nkeys = k_ref.shape[1]
    dh = DIFF_HEAD_DIM

    @pl.when(qi == 0)
    def _():
        vp_ref[:, :LANES] = v_ref[0]
        vp_ref[:, LANES:] = jnp.ones((nkeys, LANES), BF16)

    lp = lam_ref[...]
    lam = (jnp.exp(jnp.sum(lp[0:1] * lp[1:2], axis=-1, keepdims=True))
           - jnp.exp(jnp.sum(lp[2:3] * lp[3:4], axis=-1, keepdims=True)) + lam_init)
    shift = (jnp.max(jnp.abs(gq_ref[...]), axis=-1, keepdims=True)
             * jnp.max(jnp.abs(gk_ref[...]), axis=-1, keepdims=True)
             * (dh * dh ** -0.5 * math.log2(math.e)))

    q = q_ref[0]
    lane = lax.broadcasted_iota(jnp.int32, q.shape, 1)
    zero = jnp.zeros_like(q)
    acc = []
    for c, qc in enumerate((jnp.where(lane < dh, q, zero), jnp.where(lane < dh, zero, q))):
        for j in range(nkeys // tk):
            s = _dot_nt(qc, k_ref[0, j * tk:(j + 1) * tk, :])
            e_ref[c, :, j * tk:(j + 1) * tk] = jnp.exp2(s - shift).astype(BF16)
        acc.append(_dot(e_ref[c], vp_ref[...]))
    a0, a1 = acc
    o = a0[:, :LANES] / a0[:, LANES:] - lam * (a1[:, :LANES] / a1[:, LANES:])
    ms = jnp.mean(o * o, axis=-1, keepdims=True)
    o = o * lax.rsqrt(ms + EPS) * sg_ref[...] * (1.0 - lam_init)
    o_ref[0] = (o * _silu(z_ref[0].astype(F32))).astype(BF16)


def _attention(qn, kn, p_big, lam_p, gq, gk, sg, ctx_len, lam_init):
    nb, ta, w = qn.shape
    seq = ta - ctx_len
    tk = 256
    tq = 1024 if seq % 1024 == 0 else 256
    assert seq % tq == 0 and seq % ctx_len == 0 and ta % tk == 0 and ctx_len % tk == 0
    kern = functools.partial(_attn_kernel, tk=tk, lam_init=lam_init)
    full = lambda a: pl.BlockSpec(a.shape, lambda b, h, i: (0,) * a.ndim)
    consts = [lam_p, gq, gk, sg]

    def call(tq_, nkeys, row_block, key_block, nq, prev):
        in_specs = [pl.BlockSpec((1, tq_, LANES), lambda b, h, i: (b, row_block + i, h)),
                    pl.BlockSpec((1, nkeys, LANES), lambda b, h, i: (b, key_block, h)),
                    pl.BlockSpec((1, nkeys, LANES), lambda b, h, i: (b, key_block, OFF_V // LANES + h)),
                    pl.BlockSpec((1, tq_, LANES), lambda b, h, i: (b, row_block + i, OFF_ZD // LANES + h))]
        in_specs += [full(a) for a in consts]
        args = [qn, kn, p_big, p_big] + consts
        aliases = {}
        if prev is not None:
            in_specs.append(pl.BlockSpec(memory_space=pl.ANY))
            aliases = {len(args): 0}
            args.append(prev)
        return pl.pallas_call(
            kern,
            grid=(nb, DIFF_HEADS, nq),
            in_specs=in_specs,
            out_specs=pl.BlockSpec((1, tq_, LANES), lambda b, h, i: (b, row_block + i, h)),
            out_shape=jax.ShapeDtypeStruct((nb, ta, w), BF16),
            scratch_shapes=[pltpu.VMEM((nkeys, 2 * LANES), BF16), pltpu.VMEM((2, tq_, nkeys), BF16)],
            input_output_aliases=aliases,
            compiler_params=_cparams(("arbitrary", "arbitrary", "arbitrary")),
            name="diff_attention" if prev is None else "diff_attention_ctx",
        )(*args)

    yd = call(tq, ta, 0, 0, seq // tq, None)
    return call(ctx_len, ctx_len, seq // ctx_len, seq // ctx_len, 1, yd)


def _merge_kernel(x_ref, mod_ref, yf_ref, yb_ref, zs_ref, yd_ref, hf_ref, hb_ref, mo_ref, mz_ref, gt_ref,
                  sng_ref, mng_ref, wb_ref, wo_ref, o_ref, *, lat_len, ctx_row):
    b, i = pl.program_id(0), pl.program_id(1)
    tm, d = x_ref.shape[1], x_ref.shape[2]

    pd = _dot(yd_ref[0], wb_ref[1])
    ya = (yf_ref[0].astype(F32) + yb_ref[0].astype(F32)) * _silu(zs_ref[0].astype(F32))
    ya = ya * lax.rsqrt(jnp.mean(ya * ya, axis=-1, keepdims=True) + EPS) * sng_ref[...]
    pa = _dot(ya.astype(BF16), wb_ref[0])

    hm = (hf_ref[0].astype(F32) + hb_ref[0].astype(F32)) * _sigmoid(mo_ref[0].astype(F32))
    parts = []
    for h in range(ML_HEADS):
        t = hm[:, h * ML_V_DIM:(h + 1) * ML_V_DIM]
        parts.append(t * lax.rsqrt(jnp.mean(t * t, axis=-1, keepdims=True) + EPS))
    yc = jnp.concatenate(parts, axis=1) * mng_ref[...] * _silu(mz_ref[0].astype(F32))

    g = _sigmoid(gt_ref[0].astype(F32))
    mixed = (g[:, :d] * pa + g[:, d:2 * d] * pd + g[:, 2 * d:] * _dot(yc.astype(BF16), wb_ref[2]))
    out = _dot(mixed.astype(BF16), wo_ref[...])

    row = i * tm + lax.broadcasted_iota(jnp.int32, (tm, 1), 0)
    is_ctx = row >= lat_len
    gate = jnp.where(is_ctx, mod_ref[pl.ds(ctx_row, 1), 2 * d:], mod_ref[pl.ds(b, 1), 2 * d:])
    o_ref[0] = x_ref[0] + gate * out


def _merge(xall, mod_l, yf, yb, p_big, yd, hf, hb, sng, mng, wb, wo, ctx_len, out_rows):
    nb, ta, d = xall.shape
    tm = _row_tile(out_rows, 512)
    kern = functools.partial(_merge_kernel, lat_len=ta - ctx_len, ctx_row=nb)
    blk = lambda col: pl.BlockSpec((1, tm, d), lambda b, i: (b, i, col))
    full = lambda a: pl.BlockSpec(a.shape, lambda b, i: (0,) * a.ndim)
    return pl.pallas_call(
        kern,
        grid=(nb, out_rows // tm),
        in_specs=[blk(0), full(mod_l), blk(0), blk(0), blk(OFF_ZS // d), blk(0), blk(0), blk(0),
                  blk(OFF_MO // d), blk(OFF_MZ // d),
                  pl.BlockSpec((1, tm, 3 * d), lambda b, i: (b, i, OFF_GATES // (3 * d))),
                  full(sng), full(mng), full(wb), full(wo)],
        out_specs=blk(0),
        out_shape=jax.ShapeDtypeStruct((nb, out_rows, d), F32),
        compiler_params=_cparams(("arbitrary", "arbitrary")),
        name="merge",
    )(xall, mod_l, yf, yb, p_big, yd, hf, hb, p_big, p_big, p_big, sng, mng, wb, wo)


def _rope_tables(ctx_len, seq):
    rows = seq // GRID_W
    row = jnp.repeat(jnp.arange(rows, dtype=F32), GRID_W)
    col = jnp.tile(jnp.arange(GRID_W, dtype=F32), rows)
    half = DIFF_HEAD_DIM // 2
    inv_freq = ROPE_BASE ** (-jnp.arange(0, half, 2, dtype=F32) / half)
    ang_r = row[:, None] * inv_freq
    ang_c = col[:, None] * inv_freq
    cos = jnp.concatenate([jnp.cos(ang_r), jnp.cos(ang_r), jnp.cos(ang_c), jnp.cos(ang_c)], axis=-1)
    sin = jnp.concatenate([jnp.sin(ang_r), jnp.sin(ang_r), jnp.sin(ang_c), jnp.sin(ang_c)], axis=-1)
    cos = jnp.concatenate([cos, jnp.ones((ctx_len, DIFF_HEAD_DIM), F32)], axis=0)
    sin = jnp.concatenate([sin, jnp.zeros((ctx_len, DIFF_HEAD_DIM), F32)], axis=0)
    cos = jnp.tile(cos, (1, LANES // DIFF_HEAD_DIM))
    sin = jnp.tile(sin, (1, LANES // DIFF_HEAD_DIM))
    first = (jnp.arange(LANES) % (DIFF_HEAD_DIM // 2)) < DIFF_HEAD_DIM // 4
    return cos, jnp.where(first, -sin, 0.0), jnp.where(first, 0.0, sin)


def _scan_constants():
    idx = np.arange(CHUNK)
    tri_f = (idx[None, :] <= idx[:, None]).astype(np.float32)
    tri = np.stack([tri_f, tri_f.T])
    trit = np.stack([tri_f.T, tri_f])
    e01 = np.zeros((2, LANES, SSD_HEADS * SSD_HEAD_DIM), np.float32)
    for d in range(2):
        for h in range(SSD_HEADS):
            e01[d, SSD_HEADS * d + h, h * SSD_HEAD_DIM:(h + 1) * SSD_HEAD_DIM] = 1.0
    half = np.arange(LANES) // DIFF_HEAD_DIM
    gm = (half[:, None] == half[None, :]).astype(np.float32)
    tri3 = np.concatenate([tri] * 3, axis=2)
    trit3 = np.concatenate([trit] * 3, axis=1)
    e01x2 = np.concatenate([e01] * 2, axis=1)
    return (jnp.asarray(tri, BF16), jnp.asarray(tri3, BF16), jnp.asarray(trit3, BF16), jnp.asarray(e01x2, BF16),
            jnp.asarray(gm, BF16))


def _pad_row(v, offset):
    v = v.reshape(1, -1).astype(F32)
    return jnp.pad(v, ((0, 0), (offset, LANES - offset - v.shape[1])))


def kernel(x, c, ctx, c_ctx, w_mod, b_mod, norm_g, w_in, ssd_conv_w, ssd_conv_b, ssd_a_log, ssd_dt_bias, ssd_d,
           ssd_norm_g, diff_qn_g, diff_kn_g, diff_lambda, diff_subln_g, ml_conv_w, ml_conv_b, ml_i_bias,
           ml_f_bias, ml_norm_g, w_branch, w_out):
    nb, seq, d = x.shape
    ctx_len = ctx.shape[1]
    depth = w_mod.shape[0]
    assert d == 1024 and ctx_len % CHUNK == 0 and seq % CHUNK == 0 and seq % GRID_W == 0 and nb < 8

    sizes = (1536, 1024, 32, 1024, 1024, 1024, 1024, 1024, 1024, 1024, 1024, 8, 8, 3072)
    offs = np.concatenate([[0], np.cumsum(sizes)])
    w_in16 = w_in.astype(BF16)
    seg = lambda n: w_in16[:, :, offs[n]:offs[n + 1]]
    w_big = jnp.concatenate([seg(1), seg(3), seg(4), seg(5), seg(6), seg(7), seg(8), seg(9), seg(10),
                             seg(13), seg(0)], axis=-1)
    w_small = jnp.concatenate([seg(2), seg(11), seg(12)], axis=-1)
    w_small = jnp.pad(w_small, ((0, 0), (0, 0), (0, LANES - w_small.shape[-1])))

    cc = jnp.concatenate([c, c_ctx[None, :], jnp.zeros((8 - nb - 1, d), F32)], axis=0)
    mod = _modulation(cc, w_mod, b_mod)

    tri, tri3, trit3, e01, gm = _scan_constants()
    cos, sa, sb = _rope_tables(ctx_len, seq)
    wb = w_branch.astype(BF16)
    wo = w_out.astype(BF16)
    xall = jnp.concatenate([x, ctx], axis=1)
    ta = seq + ctx_len

    for l in range(depth):
        lam_init = 0.8 - 0.6 * math.exp(-0.3 * l)
        p_big, p_small = _inproj(xall, mod[l], norm_g[l][None, :], w_big[l], w_small[l], ctx_len)
        u_ssd, u_ml = _conv(p_big, ssd_conv_w[l], ssd_conv_b[l][None, :], ml_conv_w[l], ml_conv_b[l][None, :],
                            ctx_len)
        yf, yb = _ssd(u_ssd, p_small, _pad_row(ssd_dt_bias[l], 0), _pad_row(-jnp.exp(ssd_a_log[l]), 0),
                      jnp.repeat(ssd_d[l], SSD_HEAD_DIM)[None, :], tri, tri3, trit3, e01, ctx_len)
        qn, kn = _qk_prep(p_big, cos, sa, sb, jnp.tile(diff_qn_g[l], 2)[None, :],
                          jnp.tile(diff_kn_g[l], 2)[None, :], gm)
        yd = _attention(qn, kn, p_big, diff_lambda[l], diff_qn_g[l][None, :], diff_kn_g[l][None, :],
                        diff_subln_g[l][None, :], ctx_len, lam_init)
        hf, hb = _mlstm(u_ml, p_big, p_small, _pad_row(ml_i_bias[l], SM_IG), _pad_row(ml_f_bias[l], SM_FG),
                        tri, trit3, ctx_len)
        xall = _merge(xall, mod[l], yf, yb, p_big, yd, hf, hb, ssd_norm_g[l][None, :], ml_norm_g[l][None, :],
                      wb[l], wo[l], ctx_len, ta if l + 1 < depth else seq)
    return xall
```
